```python
import math
import jax
import jax.numpy as jnp
from jax import lax
import numpy as np

D_MODEL = 1024
BATCH = 16
SEQ = 2048
DEPTH = 4
DEC_BATCH = 16
DEC_SEQ = 64
PAST_LEN = 1024

CHUNK = 64
Q_BLOCK = 128
N_EVEN = (DEPTH + 1) // 2
N_ODD = DEPTH // 2
DEEP_ALPHA = (2.0 * DEPTH) ** 0.25
DEEP_BETA = (8.0 * DEPTH) ** -0.25
EPS = 1e-6
D_FF = 2816

GDN_H = 4
GDN_DK = 128
GDN_DV = 128
GDN_CONV = 4
GDN_QKV = GDN_H * (2 * GDN_DK + GDN_DV)

MLA_H = 4
MLA_NOPE = 128
MLA_ROPE = 64
MLA_V = 128
MLA_Q_LORA = 256
MLA_KV_LORA = 128
MLA_SCALE = (MLA_NOPE + MLA_ROPE) ** -0.5
ROPE_THETA = 10000.0

OFF_Z = GDN_QKV
OFF_B = OFF_Z + GDN_H * GDN_DV
OFF_A = OFF_B + GDN_H
OFF_QA = OFF_A + GDN_H
OFF_KVA = OFF_QA + MLA_Q_LORA
IN_COLS = OFF_KVA + MLA_KV_LORA + MLA_ROPE
MIX_OUT = GDN_H * GDN_DV + MLA_H * MLA_V

S5_M = 16
S5_G = D_MODEL // S5_M
S5_P = 64

kernel_name = 'hybrid_streaming_gdn_mla_s5'


def layer_norm(x, g, b):
    xf = x.astype(jnp.float32)
    mu = jnp.mean(xf, axis=-1, keepdims=True)
    var = jnp.mean(jnp.square(xf - mu), axis=-1, keepdims=True)
    y = (xf - mu) * lax.rsqrt(var + EPS) * g.astype(jnp.float32) + b.astype(jnp.float32)
    return y.astype(x.dtype)


def rms_norm(x, g):
    xf = x.astype(jnp.float32)
    y = xf * lax.rsqrt(jnp.mean(xf * xf, axis=-1, keepdims=True) + EPS) * g.astype(jnp.float32)
    return y.astype(x.dtype)


def l2_norm(x):
    xf = x.astype(jnp.float32)
    return xf * lax.rsqrt(jnp.sum(xf * xf, axis=-1, keepdims=True) + EPS)


def swiglu(u, w_in, w_out):
    gate, up = jnp.split(u @ w_in, 2, axis=-1)
    return (jax.nn.silu(gate) * up) @ w_out


def rope(x, pos):
    half = MLA_ROPE // 2
    inv = ROPE_THETA ** (-jnp.arange(half, dtype=jnp.float32) / half)
    ang = pos.astype(jnp.float32)[:, None] * inv[None, :]
    shape = (ang.shape[0],) + (1,) * (x.ndim - 3) + (half,)
    cos, sin = jnp.cos(ang).reshape(shape), jnp.sin(ang).reshape(shape)
    xf = x.astype(jnp.float32)
    x1, x2 = xf[..., :half], xf[..., half:]
    return jnp.concatenate([x1 * cos - x2 * sin, x2 * cos + x1 * sin], axis=-1).astype(x.dtype)


def causal_conv(buf, x, w):
    xp = jnp.concatenate([buf.astype(x.dtype), x], axis=1)
    L = x.shape[1]
    y = sum(xp[:, i:i + L] * w[i] for i in range(GDN_CONV))
    return y, xp[:, xp.shape[1] - (GDN_CONV - 1):]


def gdn_chunk_step(s0, inp):
    q, k, v, beta, g = inp
    C = q.shape[2]
    G = jnp.cumsum(g, axis=-1)
    incl = jnp.tril(jnp.ones((C, C), dtype=bool))
    strict = jnp.tril(jnp.ones((C, C), dtype=bool), -1)
    decay = jnp.where(incl, jnp.exp(jnp.where(incl, G[..., :, None] - G[..., None, :], 0.0)), 0.0)
    lower = jnp.where(strict, beta[..., :, None] * decay * jnp.einsum('bhtd,bhsd->bhts', k, k), 0.0)
    gam = jnp.exp(G)
    rhs = jnp.concatenate([beta[..., None] * v, (beta * gam)[..., None] * k], axis=-1)
    sol = lax.linalg.triangular_solve(lower + jnp.eye(C, dtype=lower.dtype), rhs,
                                      left_side=True, lower=True, unit_diagonal=True)
    dv = v.shape[-1]
    u = sol[..., :dv] - jnp.einsum('bhtk,bhkv->bhtv', sol[..., dv:], s0)
    o = (gam[..., None] * jnp.einsum('bhtk,bhkv->bhtv', q, s0)
         + jnp.einsum('bhts,bhsv->bhtv', jnp.einsum('bhtd,bhsd->bhts', q, k) * decay, u))
    g_end = G[..., -1:]
    s_new = (jnp.exp(g_end)[..., None] * s0
             + jnp.einsum('bhsk,bhsv->bhkv', jnp.exp(g_end - G)[..., None] * k, u))
    return s_new, o


def gated_delta(q, k, v, beta, g, s0):
    B, L, H = q.shape[:3]
    C = min(CHUNK, L)
    n = L // C

    def to_chunks(t):
        t = t.reshape(B, n, C, *t.shape[2:])
        return jnp.moveaxis(jnp.moveaxis(t, 3, 2), 1, 0)

    s_new, o = lax.scan(gdn_chunk_step, s0, tuple(to_chunks(t) for t in (q, k, v, beta, g)))
    o = jnp.moveaxis(jnp.moveaxis(o, 0, 1), 2, 3).reshape(B, L, H, o.shape[-1])
    return o, s_new


def mla_block(q_nope, q_rope, q_pos, k_nope, k_rope, v, k_pos):
    s = (jnp.einsum('bqhd,bkhd->bhqk', q_nope, k_nope)
         + jnp.einsum('bqhr,bkr->bhqk', q_rope, k_rope)).astype(jnp.float32) * MLA_SCALE
    visible = (q_pos // CHUNK)[:, None] >= (k_pos // CHUNK)[None, :]
    p = jax.nn.softmax(jnp.where(visible, s, -jnp.inf), axis=-1).astype(v.dtype)
    return jnp.einsum('bhqk,bkhd->bqhd', p, v)


def mla_attend(q_nope, q_rope, q_pos, k_nope, k_rope, v, k_pos):
    B, L = q_nope.shape[:2]
    if L <= Q_BLOCK:
        return mla_block(q_nope, q_rope, q_pos, k_nope, k_rope, v, k_pos)
    nb = L // Q_BLOCK

    def blocks(t):
        return jnp.moveaxis(t.reshape(B, nb, Q_BLOCK, *t.shape[2:]), 1, 0)

    out = lax.map(lambda a: mla_block(a[0], a[1], a[2], k_nope, k_rope, v, k_pos),
                  (blocks(q_nope), blocks(q_rope), q_pos.reshape(nb, Q_BLOCK)))
    return jnp.moveaxis(out, 0, 1).reshape(B, L, MLA_H, MLA_V)


def even_mixer(u, pos, s0, conv0, past_ckv, past_kr, w_in, conv_w, a_log, dt_bias, norm_w,
               q_norm, w_q_b, kv_norm, w_kv_b, w_out):
    f32 = jnp.float32
    B, L, _ = u.shape
    proj = u @ w_in
    qkv, conv_new = causal_conv(conv0, proj[..., :GDN_QKV], conv_w)
    qkv = jax.nn.silu(qkv)
    nq = GDN_H * GDN_DK
    q = l2_norm(qkv[..., :nq].reshape(B, L, GDN_H, GDN_DK)) * (GDN_DK ** -0.5)
    k = l2_norm(qkv[..., nq:2 * nq].reshape(B, L, GDN_H, GDN_DK))
    v = qkv[..., 2 * nq:].reshape(B, L, GDN_H, GDN_DV).astype(f32)
    z = proj[..., OFF_Z:OFF_B].reshape(B, L, GDN_H, GDN_DV)
    beta = jax.nn.sigmoid(proj[..., OFF_B:OFF_A].astype(f32))
    g = -jnp.exp(a_log.astype(f32)) * jax.nn.softplus(proj[..., OFF_A:OFF_QA].astype(f32) + dt_bias.astype(f32))
    o, s_new = gated_delta(q, k, v, beta, g, s0.astype(f32))
    o_gdn = (rms_norm(o, norm_w).astype(u.dtype) * jax.nn.silu(z)).reshape(B, L, GDN_H * GDN_DV)
    cq = rms_norm(proj[..., OFF_QA:OFF_KVA], q_norm)
    qh = (cq @ w_q_b).reshape(B, L, MLA_H, MLA_NOPE + MLA_ROPE)
    q_nope, q_rope = qh[..., :MLA_NOPE], rope(qh[..., MLA_NOPE:], pos)
    ckv = rms_norm(proj[..., OFF_KVA:OFF_KVA + MLA_KV_LORA], kv_norm)
    kr = rope(proj[..., OFF_KVA + MLA_KV_LORA:], pos)
    if past_ckv is None:
        keys_ckv, keys_kr, k_pos = ckv, kr, pos
    else:
        keys_ckv = jnp.concatenate([past_ckv.astype(ckv.dtype), ckv], axis=1)
        keys_kr = jnp.concatenate([past_kr.astype(kr.dtype), kr], axis=1)
        k_pos = jnp.arange(keys_ckv.shape[1], dtype=jnp.int32)
    kv = (keys_ckv @ w_kv_b).reshape(B, keys_ckv.shape[1], MLA_H, MLA_NOPE + MLA_V)
    o_mla = mla_attend(q_nope, q_rope, pos, kv[..., :MLA_NOPE], keys_kr, kv[..., MLA_NOPE:], k_pos)
    o_mla = o_mla.reshape(B, L, MLA_H * MLA_V)
    out = jnp.concatenate([o_gdn, o_mla], axis=-1) @ w_out
    return out, s_new, conv_new, ckv, kr


def complex_affine_combine(e1, e2):
    a1r, a1i, b1r, b1i = e1
    a2r, a2i, b2r, b2i = e2
    return (a2r * a1r - a2i * a1i, a2r * a1i + a2i * a1r,
            a2r * b1r - a2i * b1i + b2r, a2r * b1i + a2i * b1r + b2i)


def s5_mixer(u, h0_re, h0_im, lam_re, lam_im, log_dt, b_re, b_im, c_re, c_im, d_skip, w_glu, w_out):
    f32 = jnp.float32
    B, L, _ = u.shape
    dt = jnp.exp(log_dt.astype(f32))[:, None]
    lr = jnp.minimum(lam_re.astype(f32), -1e-4)
    li = lam_im.astype(f32)
    mag = jnp.exp(lr * dt)
    ab_re, ab_im = mag * jnp.cos(li * dt), mag * jnp.sin(li * dt)
    den = lr * lr + li * li
    coef_re = ((ab_re - 1.0) * lr + ab_im * li) / den
    coef_im = (ab_im * lr - (ab_re - 1.0) * li) / den
    br, bi = b_re.astype(f32), b_im.astype(f32)
    bb_re = coef_re[..., None] * br - coef_im[..., None] * bi
    bb_im = coef_re[..., None] * bi + coef_im[..., None] * br
    cr, ci = c_re.astype(f32), c_im.astype(f32)
    C = min(CHUNK, L)
    n = L // C
    ug = jnp.moveaxis(u.astype(f32).reshape(B, n, C, S5_G, S5_M), 1, 0)

    def step(h, u_c):
        h_re, h_im = h
        bu_re = jnp.einsum('bcgm,gpm->bcgp', u_c, bb_re)
        bu_im = jnp.einsum('bcgm,gpm->bcgp', u_c, bb_im)
        a_re = jnp.broadcast_to(ab_re, bu_re.shape)
        a_im = jnp.broadcast_to(ab_im, bu_re.shape)
        A_re, A_im, s_re, s_im = lax.associative_scan(complex_affine_combine, (a_re, a_im, bu_re, bu_im), axis=1)
        x_re = A_re * h_re[:, None] - A_im * h_im[:, None] + s_re
        x_im = A_re * h_im[:, None] + A_im * h_re[:, None] + s_im
        y = jnp.einsum('gmp,bcgp->bcgm', cr, x_re) - jnp.einsum('gmp,bcgp->bcgm', ci, x_im)
        return (x_re[:, -1], x_im[:, -1]), y

    (h_re, h_im), ys = lax.scan(step, (h0_re.astype(f32), h0_im.astype(f32)), ug)
    y = jnp.moveaxis(ys, 0, 1).reshape(B, L, D_MODEL).astype(u.dtype) + d_skip * u
    yg = jax.nn.gelu(y)
    out = (yg * jax.nn.sigmoid(yg @ w_glu)) @ w_out
    return out, h_re, h_im


def setup_inputs(seed: int = 0) -> dict:
    key = jax.random.key(seed)
    ks = iter(jax.random.split(key, 48))

    def nrm(shape, std=1.0):
        return std * jax.random.normal(next(ks), shape, jnp.float32)

    def uni(shape, lo, hi):
        return jax.random.uniform(next(ks), shape, jnp.float32, lo, hi)

    dt0 = jnp.exp(uni((N_EVEN, GDN_H), math.log(1e-3), math.log(1e-1)))
    lam_im0 = math.pi * jnp.arange(S5_P, dtype=jnp.float32)
    return {
        'x_prompt': nrm((BATCH, SEQ, D_MODEL)),
        'x_sample': nrm((DEC_BATCH, DEC_SEQ, D_MODEL)),
        'c_prompt': nrm((BATCH, D_MODEL)),
        'c_sample': nrm((DEC_BATCH, D_MODEL)),
        'cache_mla_latent': nrm((N_EVEN, DEC_BATCH, PAST_LEN, MLA_KV_LORA)),
        'cache_mla_krope': nrm((N_EVEN, DEC_BATCH, PAST_LEN, MLA_ROPE)),
        'state_gdn': nrm((N_EVEN, DEC_BATCH, GDN_H, GDN_DK, GDN_DV), 0.3),
        'state_gdn_conv': nrm((N_EVEN, DEC_BATCH, GDN_CONV - 1, GDN_QKV)),
        'state_s5_re': nrm((N_ODD, DEC_BATCH, S5_G, S5_P), 0.3),
        'state_s5_im': nrm((N_ODD, DEC_BATCH, S5_G, S5_P), 0.3),
        'w_ada': nrm((DEPTH, D_MODEL, 9 * D_MODEL), 0.2 * D_MODEL ** -0.5),
        'b_ada': nrm((DEPTH, 9 * D_MODEL), 0.01),
        'ln_g': 1.0 + nrm((DEPTH, 3, D_MODEL), 0.01),
        'ln_b': nrm((DEPTH, 3, D_MODEL), 0.01),
        'w_ff_in': nrm((DEPTH, 2, D_MODEL, 2 * D_FF), D_MODEL ** -0.5),
        'w_ff_out': nrm((DEPTH, 2, D_FF, D_MODEL), DEEP_BETA * D_FF ** -0.5),
        'w_mix_in': nrm((N_EVEN, D_MODEL, IN_COLS), D_MODEL ** -0.5),
        'gdn_conv_w': nrm((N_EVEN, GDN_CONV, GDN_QKV), GDN_CONV ** -0.5),
        'gdn_a_log': jnp.log(uni((N_EVEN, GDN_H), 1.0, 16.0)),
        'gdn_dt_bias': dt0 + jnp.log(-jnp.expm1(-dt0)),
        'gdn_norm_w': 1.0 + nrm((N_EVEN, GDN_DV), 0.01),
        'mla_q_norm': 1.0 + nrm((N_EVEN, MLA_Q_LORA), 0.01),
        'mla_w_q_b': nrm((N_EVEN, MLA_Q_LORA, MLA_H * (MLA_NOPE + MLA_ROPE)), MLA_Q_LORA ** -0.5),
        'mla_kv_norm': 1.0 + nrm((N_EVEN, MLA_KV_LORA), 0.01),
        'mla_w_kv_b': nrm((N_EVEN, MLA_KV_LORA, MLA_H * (MLA_NOPE + MLA_V)), MLA_KV_LORA ** -0.5),
        'w_mix_out': nrm((N_EVEN, MIX_OUT, D_MODEL), DEEP_BETA * MIX_OUT ** -0.5),
        's5_lambda_re': -0.5 + nrm((N_ODD, S5_G, S5_P), 0.01),
        's5_lambda_im': lam_im0 + nrm((N_ODD, S5_G, S5_P), 0.01),
        's5_log_dt': uni((N_ODD, S5_G), math.log(1e-3), math.log(1e-1)),
        's5_b_re': nrm((N_ODD, S5_G, S5_P, S5_M), (2 * S5_M) ** -0.5),
        's5_b_im': nrm((N_ODD, S5_G, S5_P, S5_M), (2 * S5_M) ** -0.5),
        's5_c_re': nrm((N_ODD, S5_G, S5_M, S5_P), S5_P ** -0.5),
        's5_c_im': nrm((N_ODD, S5_G, S5_M, S5_P), S5_P ** -0.5),
        's5_d': nrm((N_ODD, D_MODEL)),
        's5_w_glu': nrm((N_ODD, D_MODEL, D_MODEL), D_MODEL ** -0.5),
        's5_w_out': nrm((N_ODD, D_MODEL, D_MODEL), DEEP_BETA * D_MODEL ** -0.5),
    }


def reference(x_prompt, x_sample, c_prompt, c_sample, cache_mla_latent, cache_mla_krope, state_gdn,
              state_gdn_conv, state_s5_re, state_s5_im, w_ada, b_ada, ln_g, ln_b, w_ff_in, w_ff_out,
              w_mix_in, gdn_conv_w, gdn_a_log, gdn_dt_bias, gdn_norm_w, mla_q_norm, mla_w_q_b,
              mla_kv_norm, mla_w_kv_b, w_mix_out, s5_lambda_re, s5_lambda_im, s5_log_dt, s5_b_re,
              s5_b_im, s5_c_re, s5_c_im, s5_d, s5_w_glu, s5_w_out):

    def run_trunk(x, c, pos, gdn0, conv0, s5re0, s5im0, past_ckv, past_kr):
        B = x.shape[0]
        sc = jax.nn.silu(c)
        new_ckv, new_kr, new_gdn, new_conv, new_re, new_im = [], [], [], [], [], []
        for l in range(DEPTH):
            mod = (sc @ w_ada[l] + b_ada[l]).reshape(B, 3, 3, 1, D_MODEL)
            shift, scale, gate = mod[:, :, 0], mod[:, :, 1], 1.0 + mod[:, :, 2]
            h = swiglu(x * (1.0 + scale[:, 0]) + shift[:, 0], w_ff_in[l, 0], w_ff_out[l, 0])
            x = layer_norm(DEEP_ALPHA * x + 0.5 * gate[:, 0] * h, ln_g[l, 0], ln_b[l, 0])
            u = x * (1.0 + scale[:, 1]) + shift[:, 1]
            i = l // 2
            if l % 2 == 0:
                h, s_new, cv_new, ckv, kr = even_mixer(
                    u, pos, gdn0[i], conv0[i],
                    None if past_ckv is None else past_ckv[i], None if past_kr is None else past_kr[i],
                    w_mix_in[i], gdn_conv_w[i], gdn_a_log[i], gdn_dt_bias[i], gdn_norm_w[i],
                    mla_q_norm[i], mla_w_q_b[i], mla_kv_norm[i], mla_w_kv_b[i], w_mix_out[i])
                new_gdn.append(s_new)
                new_conv.append(cv_new)
                new_ckv.append(ckv)
                new_kr.append(kr)
            else:
                h, h_re, h_im = s5_mixer(
                    u, s5re0[i], s5im0[i], s5_lambda_re[i], s5_lambda_im[i], s5_log_dt[i],
                    s5_b_re[i], s5_b_im[i], s5_c_re[i], s5_c_im[i], s5_d[i], s5_w_glu[i], s5_w_out[i])
                new_re.append(h_re)
                new_im.append(h_im)
            x = layer_norm(DEEP_ALPHA * x + gate[:, 1] * h, ln_g[l, 1], ln_b[l, 1])
            h = swiglu(x * (1.0 + scale[:, 2]) + shift[:, 2], w_ff_in[l, 1], w_ff_out[l, 1])
            x = layer_norm(DEEP_ALPHA * x + 0.5 * gate[:, 2] * h, ln_g[l, 2], ln_b[l, 2])
        dt = x.dtype
        return (x, jnp.stack(new_ckv), jnp.stack(new_kr), jnp.stack(new_gdn).astype(dt),
                jnp.stack(new_conv), jnp.stack(new_re).astype(dt), jnp.stack(new_im).astype(dt))

    f32 = jnp.float32
    bp, sp = x_prompt.shape[0], x_prompt.shape[1]
    y_prompt, lat_p, kr_p, gdn_p, conv_p, re_p, im_p = run_trunk(
        x_prompt, c_prompt, jnp.arange(sp, dtype=jnp.int32),
        jnp.zeros((N_EVEN, bp, GDN_H, GDN_DK, GDN_DV), f32),
        jnp.zeros((N_EVEN, bp, GDN_CONV - 1, GDN_QKV), x_prompt.dtype),
        jnp.zeros((N_ODD, bp, S5_G, S5_P), f32), jnp.zeros((N_ODD, bp, S5_G, S5_P), f32),
        None, None)
    past = cache_mla_latent.shape[2]
    y_sample, lat_s, kr_s, gdn_s, conv_s, re_s, im_s = run_trunk(
        x_sample, c_sample, past + jnp.arange(x_sample.shape[1], dtype=jnp.int32),
        state_gdn, state_gdn_conv, state_s5_re, state_s5_im, cache_mla_latent, cache_mla_krope)
    return (y_prompt, y_sample, lat_p, kr_p, gdn_p, conv_p, re_p, im_p,
            lat_s, kr_s, gdn_s, conv_s, re_s, im_s)
```

```python
import functools
import math

import jax
import jax.numpy as jnp
from jax import lax
from jax.experimental import pallas as pl
from jax.experimental.pallas import tpu as pltpu

F32 = jnp.float32
BF16 = jnp.bfloat16

D_MODEL = 1024
DEPTH = 4
CHUNK = 64
N_EVEN = (DEPTH + 1) // 2
N_ODD = DEPTH // 2
DEEP_ALPHA = (2.0 * DEPTH) ** 0.25
EPS = 1e-6
D_FF = 2816

GDN_H = 4
GDN_DK = 128
GDN_DV = 128
GDN_CONV = 4
GDN_QKV = GDN_H * (2 * GDN_DK + GDN_DV)

MLA_H = 4
MLA_NOPE = 128
MLA_ROPE = 64
MLA_V = 128
MLA_Q_LORA = 256
MLA_KV_LORA = 128
MLA_SCALE = (MLA_NOPE + MLA_ROPE) ** -0.5
ROPE_THETA = 10000.0

OFF_Z = GDN_QKV
OFF_B = OFF_Z + GDN_H * GDN_DV
OFF_A = OFF_B + GDN_H
OFF_QA = OFF_A + GDN_H
OFF_KVA = OFF_QA + MLA_Q_LORA

S5_M = 16
S5_G = D_MODEL // S5_M
S5_P = 64
S5_STATE = S5_G * S5_P
S5_KT = 256
S5_NKT = D_MODEL // S5_KT
S5_GPT = S5_KT // S5_M
S5_SPT = S5_GPT * S5_P

LANE = 128
SUBLANE = 8
VMEM_LIMIT = 56 * 1024 * 1024

FF_CHUNK = 256
FF_NCH = D_FF // FF_CHUNK
GDN_C = 128

E1_QKV = 0
E1_Z = GDN_QKV
E1_QA = E1_Z + GDN_H * GDN_DV
E1_CKV = E1_QA + MLA_Q_LORA
E1_KRA = E1_CKV + MLA_KV_LORA
E1_KRB = E1_KRA + LANE
E1_BA = E1_KRB + LANE
E1_COLS = E1_BA + LANE


def _cparams(sem):
    return pltpu.CompilerParams(dimension_semantics=sem, vmem_limit_bytes=VMEM_LIMIT)


def _dot(a, b):
    return jnp.dot(a, b, preferred_element_type=F32)


def _dot_nt(a, b):
    return lax.dot_general(a, b, (((1,), (1,)), ((), ())), preferred_element_type=F32)


def _sigmoid(x):
    return 1.0 / (1.0 + jnp.exp(-x))


def _silu(x):
    return x * _sigmoid(x)


def _layer_norm(y, g, b):
    mu = jnp.mean(y, axis=-1, keepdims=True)
    yc = y - mu
    var = jnp.mean(yc * yc, axis=-1, keepdims=True)
    return yc * lax.rsqrt(var + EPS) * g + b


def _rms_norm(y, g):
    return y * lax.rsqrt(jnp.mean(y * y, axis=-1, keepdims=True) + EPS) * g


def _split3(a):
    h = a.astype(BF16)
    r = a - h.astype(F32)
    m = r.astype(BF16)
    l = (r - m.astype(F32)).astype(BF16)
    return h, m, l


def _mm3(a, b):
    ah = a.astype(BF16)
    al = (a - ah.astype(F32)).astype(BF16)
    bh = b.astype(BF16)
    bl = (b - bh.astype(F32)).astype(BF16)
    return _dot(ah, bh) + _dot(ah, bl) + _dot(al, bh)


def _ada_kernel(c_ref, w_ref, b_ref, o_ref):
    sc = _silu(c_ref[...]).astype(BF16)
    o_ref[0] = _dot(sc, w_ref[0].astype(BF16)) + b_ref[0]


def _ada_call(c_all, w_ada, b_ada):
    n = c_all.shape[0]
    ncol = w_ada.shape[-1] // D_MODEL
    return pl.pallas_call(
        _ada_kernel,
        grid=(DEPTH, ncol),
        in_specs=[
            pl.BlockSpec((n, D_MODEL), lambda l, j: (0, 0)),
            pl.BlockSpec((1, D_MODEL, D_MODEL), lambda l, j: (l, 0, j)),
            pl.BlockSpec((1, 1, D_MODEL), lambda l, j: (l, 0, j)),
        ],
        out_specs=pl.BlockSpec((1, n, D_MODEL), lambda l, j: (l, 0, j)),
        out_shape=jax.ShapeDtypeStruct((DEPTH, n, ncol * D_MODEL), F32),
        compiler_params=_cparams(("parallel", "parallel")),
        name="ada_mod",
    )(c_all, w_ada, b_ada.reshape(DEPTH, 1, -1))


def _mod_spec(mod3, layer, sub, part, off, bb):
    base = (layer * (mod3.shape[0] // DEPTH) + off) // bb
    col = sub * 3 + part
    return pl.BlockSpec((bb, 1, D_MODEL), lambda i, t: (base + i, 0, col))


def _ffn_kernel(x_ref, sh_ref, sc_ref, gt_ref, win_ref, wout_ref, lng_ref, lnb_ref, o_ref, acc_ref):
    bb, tt, _ = x_ref.shape
    rows = bb * tt
    x = x_ref[...]
    u = x * (1.0 + sc_ref[...]) + sh_ref[...]
    ub = u.reshape(rows, D_MODEL).astype(BF16)
    acc_ref[...] = jnp.zeros_like(acc_ref)

    def step(c, carry):
        gu = _dot(ub, win_ref[c])
        h = (_silu(gu[:, :FF_CHUNK]) * gu[:, FF_CHUNK:]).astype(BF16)
        acc_ref[...] += _dot(h, wout_ref[c])
        return carry

    lax.fori_loop(0, FF_NCH, step, 0)
    h = acc_ref[...].reshape(bb, tt, D_MODEL)
    y = DEEP_ALPHA * x + 0.5 * (1.0 + gt_ref[...]) * h
    o_ref[...] = _layer_norm(y, lng_ref[...], lnb_ref[...])


def _ffn_call(x, mod3, layer, sub, off, win_r, wout_r, lng, lnb, bb, tt):
    nb, T, _ = x.shape
    const3 = lambda i, t: (0, 0, 0)
    const2 = lambda i, t: (0, 0)
    return pl.pallas_call(
        _ffn_kernel,
        grid=(nb // bb, T // tt),
        in_specs=[
            pl.BlockSpec((bb, tt, D_MODEL), lambda i, t: (i, t, 0)),
            _mod_spec(mod3, layer, sub, 0, off, bb),
            _mod_spec(mod3, layer, sub, 1, off, bb),
            _mod_spec(mod3, layer, sub, 2, off, bb),
            pl.BlockSpec(win_r.shape, const3, pipeline_mode=pl.Buffered(1)),
            pl.BlockSpec(wout_r.shape, const3, pipeline_mode=pl.Buffered(1)),
            pl.BlockSpec((1, D_MODEL), const2),
            pl.BlockSpec((1, D_MODEL), const2),
        ],
        out_specs=pl.BlockSpec((bb, tt, D_MODEL), lambda i, t: (i, t, 0)),
        out_shape=jax.ShapeDtypeStruct(x.shape, F32),
        scratch_shapes=[pltpu.VMEM((bb * tt, D_MODEL), F32)],
        compiler_params=_cparams(("parallel", "parallel")),
        name="ffn",
    )(x, mod3, mod3, mod3, win_r, wout_r, lng, lnb)


def _e1_kernel(x_ref, sh_ref, sc_ref, w_ref, conv0_ref, convw_ref, cos_ref, sin_ref, qn_ref, kvn_ref,
               wqa_ref, wqb_ref, wkn_ref, wv_ref,
               qkvn_ref, z_ref, bg_ref, qa_ref, ka_ref, vv_ref, ckv_ref, kr_ref, tail_ref, tail_sc):
    bb, tt, _ = x_ref.shape
    rows = bb * tt

    @pl.when(pl.program_id(1) == 0)
    def _():
        tail_sc[...] = conv0_ref[...]

    u = x_ref[...] * (1.0 + sc_ref[...]) + sh_ref[...]
    ub = u.reshape(rows, D_MODEL).astype(BF16)

    pre = _dot(ub, w_ref[:, E1_QKV:E1_Z])
    for b in range(bb):
        pre_b = pre[b * tt:(b + 1) * tt]
        hist = jnp.concatenate([tail_sc[b], pre_b], axis=0)
        y = pre_b * convw_ref[GDN_CONV - 1:GDN_CONV, :]
        for lag in range(1, GDN_CONV):
            w_row = convw_ref[GDN_CONV - 1 - lag:GDN_CONV - lag, :]
            y = y + pltpu.roll(hist, lag, 0)[SUBLANE:] * w_row
        tail_sc[b] = pre_b[tt - SUBLANE:]
        y = _silu(y)
        for h in range(GDN_H):
            qh = y[:, h * GDN_DK:(h + 1) * GDN_DK]
            qkvn_ref[b, :, h * GDN_DK:(h + 1) * GDN_DK] = (
                qh * lax.rsqrt(jnp.sum(qh * qh, axis=-1, keepdims=True) + EPS) * (GDN_DK ** -0.5))
            ko = GDN_H * GDN_DK + h * GDN_DK
            kh = y[:, ko:ko + GDN_DK]
            qkvn_ref[b, :, ko:ko + GDN_DK] = kh * lax.rsqrt(jnp.sum(kh * kh, axis=-1, keepdims=True) + EPS)
        vo = 2 * GDN_H * GDN_DK
        qkvn_ref[b, :, vo:] = y[:, vo:]
    tail_ref[...] = tail_sc[...]

    z_ref[...] = _dot(ub, w_ref[:, E1_Z:E1_QA]).reshape(bb, tt, -1)
    bg_ref[...] = _dot(ub, w_ref[:, E1_BA:E1_COLS]).reshape(bb, tt, -1)

    cos = cos_ref[...]
    sin = sin_ref[...]
    if bb > 1:
        cos = jnp.concatenate([cos] * bb, axis=0)
        sin = jnp.concatenate([sin] * bb, axis=0)

    cq = _rms_norm(_dot(ub, w_ref[:, E1_QA:E1_CKV]), qn_ref[...]).astype(BF16)
    qa = _dot(cq, wqa_ref[...])
    qb = _dot(cq, wqb_ref[...])
    for h in range(MLA_H):
        o = 2 * LANE * h
        qa_ref[:, :, o:o + LANE] = qa[:, o:o + LANE].astype(BF16).reshape(bb, tt, LANE)
        rot = qa[:, o + LANE:o + 2 * LANE] * cos + qb[:, h * LANE:(h + 1) * LANE] * sin
        qa_ref[:, :, o + LANE:o + 2 * LANE] = rot.astype(BF16).reshape(bb, tt, LANE)

    ckv = _rms_norm(_dot(ub, w_ref[:, E1_CKV:E1_KRA]), kvn_ref[...])
    ckv_ref[...] = ckv.reshape(bb, tt, -1)
    kr = _dot(ub, w_ref[:, E1_KRA:E1_KRB]) * cos + _dot(ub, w_ref[:, E1_KRB:E1_BA]) * sin
    kr_ref[...] = kr[:, :MLA_ROPE].reshape(bb, tt, -1)
    krb = kr.astype(BF16).reshape(bb, tt, LANE)
    cb = ckv.astype(BF16)
    kn = _dot(cb, wkn_ref[...])
    for h in range(MLA_H):
        o = 2 * LANE * h
        ka_ref[:, :, o:o + LANE] = kn[:, h * LANE:(h + 1) * LANE].astype(BF16).reshape(bb, tt, LANE)
        ka_ref[:, :, o + LANE:o + 2 * LANE] = krb
    vv_ref[...] = _dot(cb, wv_ref[...]).astype(BF16).reshape(bb, tt, -1)


def _e1_call(x, mod3, layer, off, ew, conv0p, cos_t, sin_t, bb, tt):
    nb, T, _ = x.shape
    tok = lambda i, t: (i, t, 0)
    const2 = lambda i, t: (0, 0)

    def full(a):
        return pl.BlockSpec(a.shape, const2)

    out_dims = [(GDN_QKV, F32), (GDN_H * GDN_DV, F32), (LANE, F32), (MLA_H * 2 * LANE, BF16),
                (MLA_H * 2 * LANE, BF16), (MLA_H * MLA_V, BF16), (MLA_KV_LORA, F32), (MLA_ROPE, F32)]
    out_shape = [jax.ShapeDtypeStruct((nb, T, d), dt) for d, dt in out_dims]
    out_specs = [pl.BlockSpec((bb, tt, d), tok) for d, _ in out_dims]
    out_shape.append(jax.ShapeDtypeStruct((nb, SUBLANE, GDN_QKV), F32))
    out_specs.append(pl.BlockSpec((bb, SUBLANE, GDN_QKV), lambda i, t: (i, 0, 0)))
    return pl.pallas_call(
        _e1_kernel,
        grid=(nb // bb, T // tt),
        in_specs=[
            pl.BlockSpec((bb, tt, D_MODEL), tok),
            _mod_spec(mod3, layer, 1, 0, off, bb),
            _mod_spec(mod3, layer, 1, 1, off, bb),
            full(ew["w_e1"]),
            pl.BlockSpec((bb, SUBLANE, GDN_QKV), lambda i, t: (i, 0, 0)),
            full(ew["conv_w"]),
            pl.BlockSpec((tt, LANE), lambda i, t: (t, 0)),
            pl.BlockSpec((tt, LANE), lambda i, t: (t, 0)),
            full(ew["q_norm"]), full(ew["kv_norm"]),
            full(ew["wqa"]), full(ew["wqb"]), full(ew["wkn"]), full(ew["wv"]),
        ],
        out_specs=out_specs,
        out_shape=out_shape,
        scratch_shapes=[pltpu.VMEM((bb, SUBLANE, GDN_QKV), F32)],
        compiler_params=_cparams(("parallel", "arbitrary")),
        name="mix_proj",
    )(x, mod3, mod3, ew["w_e1"], conv0p, ew["conv_w"], cos_t, sin_t, ew["q_norm"], ew["kv_norm"],
      ew["wqa"], ew["wqb"], ew["wkn"], ew["wv"])


def _gdn_kernel(qkvn_ref, z_ref, bg_ref, s0_ref, alog_ref, dtb_ref, nw_ref, o_ref, s_ref, s_sc, *, n_valid):
    bb = qkvn_ref.shape[0]
    C = GDN_C

    @pl.when(pl.program_id(1) == 0)
    def _():
        s_sc[...] = s0_ref[...]

    row = lax.broadcasted_iota(jnp.int32, (C, C), 0)
    col = lax.broadcasted_iota(jnp.int32, (C, C), 1)
    incl = row >= col
    strict = row > col
    tri = jnp.where(incl, 1.0, 0.0).astype(BF16)
    eye = jnp.where(row == col, 1.0, 0.0)

    for b in range(bb):
        bg = bg_ref[b]
        beta_all = _sigmoid(bg)
        xs = bg + dtb_ref[...]
        softplus = jnp.maximum(xs, 0.0) + jnp.log(1.0 + jnp.exp(-jnp.abs(xs)))
        g_all = -jnp.exp(alog_ref[...]) * softplus
        if n_valid < C:
            g_all = jnp.where(row < n_valid, g_all, 0.0)
        gh, gm, gl = _split3(g_all)
        G_all = _dot(tri, gh) + _dot(tri, gm) + _dot(tri, gl)
        G_t = G_all.T
        for h in range(GDN_H):
            q = qkvn_ref[b, :, h * GDN_DK:(h + 1) * GDN_DK]
            k = qkvn_ref[b, :, (GDN_H + h) * GDN_DK:(GDN_H + h + 1) * GDN_DK]
            v = qkvn_ref[b, :, 2 * GDN_H * GDN_DK + h * GDN_DV:2 * GDN_H * GDN_DK + (h + 1) * GDN_DV]
            beta = beta_all[:, h:h + 1]
            Gc = G_all[:, GDN_H + h:GDN_H + h + 1]
            Gr = G_t[GDN_H + h:GDN_H + h + 1, :]
            decay = jnp.where(incl, jnp.exp(jnp.where(incl, Gc - Gr, 0.0)), 0.0)
            kb = k.astype(BF16)
            lower = jnp.where(strict, beta * decay * _dot_nt(kb, kb), 0.0)
            inv = eye - lower
            pw = _mm3(lower, lower)
            n_sq = int(math.log2(C)) - 1
            for it in range(n_sq):
                inv = inv + _mm3(inv, pw)
                if it + 1 < n_sq:
                    pw = _mm3(pw, pw)
            gam = jnp.exp(Gc)
            sol_v = _mm3(inv, beta * v)
            sol_k = _mm3(inv, (beta * gam) * k)
            s0 = s_sc[b, h]
            s0b = s0.astype(BF16)
            u = sol_v - _dot(sol_k.astype(BF16), s0b)
            ub = u.astype(BF16)
            attn = (_dot_nt(q.astype(BF16), kb) * decay).astype(BF16)
            o = _dot((gam * q).astype(BF16), s0b) + _dot(attn, ub)
            g_end = Gc[C - 1:C, :]
            kd = (jnp.exp(g_end - Gc) * k).T.astype(BF16)
            s_sc[b, h] = jnp.exp(g_end) * s0 + _dot(kd, ub)
            zh = z_ref[b, :, h * GDN_DV:(h + 1) * GDN_DV]
            o_ref[b, :, h * GDN_DV:(h + 1) * GDN_DV] = (_rms_norm(o, nw_ref[...]) * _silu(zh)).astype(BF16)
    s_ref[...] = s_sc[...]


def _gdn_call(qkvn, z, bg, s0, ew, bb, n_valid):
    nb, T, _ = qkvn.shape
    tok = lambda i, c: (i, c, 0)
    const2 = lambda i, c: (0, 0)
    st = lambda i, c: (i, 0, 0, 0)
    return pl.pallas_call(
        functools.partial(_gdn_kernel, n_valid=n_valid),
        grid=(nb // bb, T // GDN_C),
        in_specs=[
            pl.BlockSpec((bb, GDN_C, GDN_QKV), tok),
            pl.BlockSpec((bb, GDN_C, GDN_H * GDN_DV), tok),
            pl.BlockSpec((bb, GDN_C, LANE), tok),
            pl.BlockSpec((bb, GDN_H, GDN_DK, GDN_DV), st),
            pl.BlockSpec((1, LANE), const2),
            pl.BlockSpec((1, LANE), const2),
            pl.BlockSpec((1, GDN_DV), const2),
        ],
        out_specs=[pl.BlockSpec((bb, GDN_C, GDN_H * GDN_DV), tok),
                   pl.BlockSpec((bb, GDN_H, GDN_DK, GDN_DV), st)],
        out_shape=[jax.ShapeDtypeStruct((nb, T, GDN_H * GDN_DV), BF16),
                   jax.ShapeDtypeStruct((nb, GDN_H, GDN_DK, GDN_DV), F32)],
        scratch_shapes=[pltpu.VMEM((bb, GDN_H, GDN_DK, GDN_DV), F32)],
        compiler_params=_cparams(("parallel", "arbitrary")),
        name="gdn",
    )(qkvn, z, bg, s0, ew["alog"], ew["dtb"], ew["norm_w"])


def _attn_update(h, q, k, v, vis, m_sc, l_sc, acc_sc):
    s = _dot_nt(q, k) * MLA_SCALE
    if vis is not None:
        s = jnp.where(vis, s, -jnp.inf)
    m_prev = m_sc[h]
    m_new = jnp.maximum(m_prev, jnp.max(s, axis=-1, keepdims=True))
    alpha = jnp.exp(m_prev - m_new)
    p = jnp.exp(s - m_new)
    l_sc[h] = alpha * l_sc[h] + jnp.sum(p, axis=-1, keepdims=True)
    acc_sc[h] = alpha * acc_sc[h] + _dot(p.astype(BF16), v)
    m_sc[h] = m_new


def _attn_init(m_sc, l_sc, acc_sc):
    m_sc[...] = jnp.full_like(m_sc, -jnp.inf)
    l_sc[...] = jnp.zeros_like(l_sc)
    acc_sc[...] = jnp.zeros_like(acc_sc)


def _mix_out(og, l_sc, acc_sc, x_ref, gt_ref, wout_ref, lng_ref, lnb_ref, o_ref):
    parts = [og] + [(acc_sc[h] / l_sc[h]).astype(BF16) for h in range(MLA_H)]
    mix = _dot(jnp.concatenate(parts, axis=-1), wout_ref[...])
    y = DEEP_ALPHA * x_ref[0] + (1.0 + gt_ref[0]) * mix
    o_ref[0] = _layer_norm(y, lng_ref[...], lnb_ref[...])


def _mla_prompt_kernel(qa_ref, ka_ref, vv_ref, og_ref, x_ref, gt_ref, wout_ref, lng_ref, lnb_ref, o_ref,
                       m_sc, l_sc, acc_sc):
    tq = qa_ref.shape[1]
    i = pl.program_id(1)
    _attn_init(m_sc, l_sc, acc_sc)
    qchunk = (i * tq + lax.broadcasted_iota(jnp.int32, (tq, tq), 0)) // CHUNK

    def body(j, carry):
        ks = pl.multiple_of(j * tq, tq)
        vis = qchunk >= (j * tq + lax.broadcasted_iota(jnp.int32, (tq, tq), 1)) // CHUNK
        for h in range(MLA_H):
            _attn_update(h, qa_ref[0, :, 2 * LANE * h:2 * LANE * (h + 1)],
                         ka_ref[0, pl.ds(ks, tq), 2 * LANE * h:2 * LANE * (h + 1)],
                         vv_ref[0, pl.ds(ks, tq), MLA_V * h:MLA_V * (h + 1)], vis, m_sc, l_sc, acc_sc)
        return carry

    lax.fori_loop(0, i + 1, body, 0)
    _mix_out(og_ref[0], l_sc, acc_sc, x_ref, gt_ref, wout_ref, lng_ref, lnb_ref, o_ref)


def _mla_sample_kernel(qa_ref, kp_ref, vp_ref, ka_ref, vv_ref, og_ref, x_ref, gt_ref, wout_ref, lng_ref, lnb_ref,
                       o_ref, m_sc, l_sc, acc_sc, *, tk):
    _attn_init(m_sc, l_sc, acc_sc)
    n_past = kp_ref.shape[1] // tk
    for h in range(MLA_H):
        q = qa_ref[0, :, 2 * LANE * h:2 * LANE * (h + 1)]
        for j in range(n_past):
            _attn_update(h, q, kp_ref[0, j * tk:(j + 1) * tk, 2 * LANE * h:2 * LANE * (h + 1)],
                         vp_ref[0, j * tk:(j + 1) * tk, MLA_V * h:MLA_V * (h + 1)], None, m_sc, l_sc, acc_sc)
        _attn_update(h, q, ka_ref[0, :, 2 * LANE * h:2 * LANE * (h + 1)],
                     vv_ref[0, :, MLA_V * h:MLA_V * (h + 1)], None, m_sc, l_sc, acc_sc)
    _mix_out(og_ref[0], l_sc, acc_sc, x_ref, gt_ref, wout_ref, lng_ref, lnb_ref, o_ref)


def _mla_scratch(tq):
    return [pltpu.VMEM((MLA_H, tq, 1), F32), pltpu.VMEM((MLA_H, tq, 1), F32), pltpu.VMEM((MLA_H, tq, MLA_V), F32)]


def _mla_prompt_call(qa, ka, vv, og, x, mod3, layer, off, w_out, lng, lnb, tq):
    nb, T, _ = x.shape
    tok = lambda b, i: (b, i, 0)
    seq = lambda b, i: (b, 0, 0)
    const2 = lambda b, i: (0, 0)
    return pl.pallas_call(
        _mla_prompt_kernel,
        grid=(nb, T // tq),
        in_specs=[
            pl.BlockSpec((1, tq, qa.shape[-1]), tok),
            pl.BlockSpec((1, T, ka.shape[-1]), seq),
            pl.BlockSpec((1, T, vv.shape[-1]), seq),
            pl.BlockSpec((1, tq, og.shape[-1]), tok),
            pl.BlockSpec((1, tq, D_MODEL), tok),
            _mod_spec(mod3, layer, 1, 2, off, 1),
            pl.BlockSpec(w_out.shape, const2),
            pl.BlockSpec((1, D_MODEL), const2),
            pl.BlockSpec((1, D_MODEL), const2),
        ],
        out_specs=pl.BlockSpec((1, tq, D_MODEL), tok),
        out_shape=jax.ShapeDtypeStruct(x.shape, F32),
        scratch_shapes=_mla_scratch(tq),
        compiler_params=_cparams(("parallel", "arbitrary")),
        name="mla_prompt",
    )(qa, ka, vv, og, x, mod3, w_out, lng, lnb)


def _mla_sample_call(qa, kp, vp, ka, vv, og, x, mod3, layer, off, w_out, lng, lnb, tk):
    nb, T, _ = x.shape
    seq = lambda b, i: (b, 0, 0)
    const2 = lambda b, i: (0, 0)

    def whole(a):
        return pl.BlockSpec((1,) + a.shape[1:], seq)

    return pl.pallas_call(
        functools.partial(_mla_sample_kernel, tk=tk),
        grid=(nb, 1),
        in_specs=[whole(qa), whole(kp), whole(vp), whole(ka), whole(vv), whole(og), whole(x),
                  _mod_spec(mod3, layer, 1, 2, off, 1),
                  pl.BlockSpec(w_out.shape, const2),
                  pl.BlockSpec((1, D_MODEL), const2),
                  pl.BlockSpec((1, D_MODEL), const2)],
        out_specs=whole(x),
        out_shape=jax.ShapeDtypeStruct(x.shape, F32),
        scratch_shapes=_mla_scratch(T),
        compiler_params=_cparams(("parallel", "arbitrary")),
        name="mla_sample",
    )(qa, kp, vp, ka, vv, og, x, mod3, w_out, lng, lnb)


def _kvpast_kernel(ckv_ref, kr_ref, wkn_ref, wv_ref, ka_ref, vv_ref):
    cb = ckv_ref[0].astype(BF16)
    kn = _dot(cb, wkn_ref[...])
    krb = kr_ref[0].astype(BF16)
    for h in range(MLA_H):
        o = 2 * LANE * h
        ka_ref[0, :, o:o + LANE] = kn[:, h * LANE:(h + 1) * LANE].astype(BF16)
        ka_ref[0, :, o + LANE:o + 2 * LANE] = krb
    vv_ref[0] = _dot(cb, wv_ref[...]).astype(BF16)


def _kvpast_call(past_ckv, past_kr128, ew):
    nb, P, _ = past_ckv.shape
    seq = lambda b: (b, 0, 0)
    const2 = lambda b: (0, 0)
    return pl.pallas_call(
        _kvpast_kernel,
        grid=(nb,),
        in_specs=[pl.BlockSpec((1, P, MLA_KV_LORA), seq), pl.BlockSpec((1, P, LANE), seq),
                  pl.BlockSpec(ew["wkn"].shape, const2), pl.BlockSpec(ew["wv"].shape, const2)],
        out_specs=[pl.BlockSpec((1, P, MLA_H * 2 * LANE), seq), pl.BlockSpec((1, P, MLA_H * MLA_V), seq)],
        out_shape=[jax.ShapeDtypeStruct((nb, P, MLA_H * 2 * LANE), BF16),
                   jax.ShapeDtypeStruct((nb, P, MLA_H * MLA_V), BF16)],
        compiler_params=_cparams(("parallel",)),
        name="mla_past_kv",
    )(past_ckv, past_kr128, ew["wkn"], ew["wv"])


def _s5prep_kernel(lre_ref, lim_ref, ldt_ref, br_ref, bi_ref, abre_ref, abim_ref, bbre_ref, bbim_ref):
    dt = jnp.exp(ldt_ref[0])
    lr = jnp.minimum(lre_ref[0], -1e-4)
    li = lim_ref[0]
    mag = jnp.exp(lr * dt)
    ab_re = mag * jnp.cos(li * dt)
    ab_im = mag * jnp.sin(li * dt)
    den = lr * lr + li * li
    coef_re = ((ab_re - 1.0) * lr + ab_im * li) / den
    coef_im = (ab_im * lr - (ab_re - 1.0) * li) / den
    br = br_ref[0]
    bi = bi_ref[0]
    abre_ref[0] = ab_re
    abim_ref[0] = ab_im
    bbre_ref[0] = coef_re * br - coef_im * bi
    bbim_ref[0] = coef_re * bi + coef_im * br


def _s5prep_call(lre_x, lim_x, ldt_x, br_t, bi_t):
    n = lre_x.shape[0]
    blk = pl.BlockSpec((1, D_MODEL, S5_P), lambda i: (i, 0, 0))
    return pl.pallas_call(
        _s5prep_kernel,
        grid=(n,),
        in_specs=[blk, blk, pl.BlockSpec((1, D_MODEL, 1), lambda i: (i, 0, 0)), blk, blk],
        out_specs=[blk] * 4,
        out_shape=[jax.ShapeDtypeStruct((n, D_MODEL, S5_P), F32)] * 4,
        compiler_params=_cparams(("parallel",)),
        name="s5_discretise",
    )(lre_x, lim_x, ldt_x, br_t, bi_t)


def _s5_kernel(x_ref, sh_ref, sc_ref, gt_ref, h0re_ref, h0im_ref, abre_ref, abim_ref, bb_ref, cre_ref, cim_ref,
               dskip_ref, wglu_ref, wout_ref, lng_ref, lnb_ref, o_ref, hre_ref, him_ref,
               hre_sc, him_sc, st_sc, y_sc):
    tt, nb, _ = x_ref.shape
    rows = tt * nb
    half = S5_SPT // 2

    @pl.when(pl.program_id(0) == 0)
    def _():
        hre_sc[...] = h0re_ref[...]
        him_sc[...] = h0im_ref[...]

    x = x_ref[...]
    u = (x * (1.0 + sc_ref[...]) + sh_ref[...]).reshape(rows, D_MODEL)
    ub = u.astype(BF16)
    for kt in range(S5_NKT):
        st_sc[...] = _dot(ub[:, kt * S5_KT:(kt + 1) * S5_KT], bb_ref[kt])
        for lc in range(2):
            so = kt * S5_SPT + lc * half
            a_re = jnp.broadcast_to(abre_ref[:, so:so + half], (nb, half))
            a_im = jnp.broadcast_to(abim_ref[:, so:so + half], (nb, half))

            def step(t, carry):
                h_re, h_im = carry
                r0 = pl.multiple_of(t * nb, nb)
                re_sl = (pl.ds(r0, nb), pl.ds(lc * half, half))
                im_sl = (pl.ds(r0, nb), pl.ds(S5_SPT + lc * half, half))
                n_re = a_re * h_re - a_im * h_im + st_sc[re_sl]
                n_im = a_re * h_im + a_im * h_re + st_sc[im_sl]
                st_sc[re_sl] = n_re
                st_sc[im_sl] = n_im
                return n_re, n_im

            h_re, h_im = lax.fori_loop(0, tt, step, (hre_sc[:, so:so + half], him_sc[:, so:so + half]))
            hre_sc[:, so:so + half] = h_re
            him_sc[:, so:so + half] = h_im
        y_sc[:, kt * S5_KT:(kt + 1) * S5_KT] = (
            _dot(st_sc[:, :S5_SPT].astype(BF16), cre_ref[kt]) - _dot(st_sc[:, S5_SPT:].astype(BF16), cim_ref[kt]))

    y = y_sc[...] + dskip_ref[...] * u
    yg = 0.5 * y * (1.0 + jnp.tanh(math.sqrt(2.0 / math.pi) * (y + 0.044715 * (y * y * y))))
    glu = _dot(yg.astype(BF16), wglu_ref[...])
    mix = _dot((yg * _sigmoid(glu)).astype(BF16), wout_ref[...])
    out = DEEP_ALPHA * x + (1.0 + gt_ref[...]) * mix.reshape(tt, nb, D_MODEL)
    o_ref[...] = _layer_norm(out, lng_ref[...], lnb_ref[...])

    @pl.when(pl.program_id(0) == pl.num_programs(0) - 1)
    def _():
        hre_ref[...] = hre_sc[...]
        him_ref[...] = him_sc[...]


def _s5_call(xt, mod, layer, stream, h0re, h0im, sw, lng, lnb, tt):
    T, nb, _ = xt.shape
    rows = tt * nb
    const2 = lambda t: (0, 0)
    const3 = lambda t: (0, 0, 0)

    def mspec(part):
        return pl.BlockSpec((1, nb, D_MODEL), lambda t: (layer, stream, 3 + part))

    def full(a):
        return pl.BlockSpec(a.shape, const2 if a.ndim == 2 else const3)

    return pl.pallas_call(
        _s5_kernel,
        grid=(T // tt,),
        in_specs=[
            pl.BlockSpec((tt, nb, D_MODEL), lambda t: (t, 0, 0)),
            mspec(0), mspec(1), mspec(2),
            full(h0re), full(h0im), full(sw["ab_re"]), full(sw["ab_im"]),
            full(sw["bb"]), full(sw["c_re"]), full(sw["c_im"]), full(sw["d_skip"]),
            full(sw["w_glu"]), full(sw["w_out"]), full(lng), full(lnb),
        ],
        out_specs=[pl.BlockSpec((tt, nb, D_MODEL), lambda t: (t, 0, 0)),
                   pl.BlockSpec((nb, S5_STATE), const2), pl.BlockSpec((nb, S5_STATE), const2)],
        out_shape=[jax.ShapeDtypeStruct(xt.shape, F32),
                   jax.ShapeDtypeStruct((nb, S5_STATE), F32), jax.ShapeDtypeStruct((nb, S5_STATE), F32)],
        scratch_shapes=[pltpu.VMEM((nb, S5_STATE), F32), pltpu.VMEM((nb, S5_STATE), F32),
                        pltpu.VMEM((rows, 2 * S5_SPT), F32), pltpu.VMEM((rows, D_MODEL), F32)],
        compiler_params=_cparams(("arbitrary",)),
        name="s5_mixer",
    )(xt, mod, mod, mod, h0re, h0im, sw["ab_re"], sw["ab_im"], sw["bb"], sw["c_re"], sw["c_im"],
      sw["d_skip"], sw["w_glu"], sw["w_out"], lng, lnb)


def _prep_ffn(w_in, w_out):
    gate = w_in[:, :D_FF].reshape(D_MODEL, FF_NCH, FF_CHUNK)
    up = w_in[:, D_FF:].reshape(D_MODEL, FF_NCH, FF_CHUNK)
    win_r = jnp.transpose(jnp.concatenate([gate, up], axis=-1), (1, 0, 2)).astype(BF16)
    return win_r, w_out.reshape(FF_NCH, FF_CHUNK, D_MODEL).astype(BF16)


def _prep_even(w_in, conv_w, a_log, dt_bias, norm_w, q_norm, w_q_b, kv_norm, w_kv_b, w_out):
    half = MLA_ROPE // 2
    zcol = lambda n: jnp.zeros((w_in.shape[0], n), w_in.dtype)
    k1 = w_in[:, OFF_KVA + MLA_KV_LORA:OFF_KVA + MLA_KV_LORA + half]
    k2 = w_in[:, OFF_KVA + MLA_KV_LORA + half:OFF_KVA + MLA_KV_LORA + MLA_ROPE]
    w_e1 = jnp.concatenate([
        w_in[:, :OFF_B],
        w_in[:, OFF_QA:OFF_KVA + MLA_KV_LORA],
        k1, k2, zcol(LANE - MLA_ROPE),
        -k2, k1, zcol(LANE - MLA_ROPE),
        w_in[:, OFF_B:OFF_QA], zcol(LANE - 2 * GDN_H),
    ], axis=1).astype(BF16)
    zq = jnp.zeros((MLA_Q_LORA, LANE - MLA_ROPE), w_q_b.dtype)
    qa_cols, qb_cols = [], []
    for h in range(MLA_H):
        o = h * (MLA_NOPE + MLA_ROPE)
        r1 = w_q_b[:, o + MLA_NOPE:o + MLA_NOPE + half]
        r2 = w_q_b[:, o + MLA_NOPE + half:o + MLA_NOPE + MLA_ROPE]
        qa_cols += [w_q_b[:, o:o + MLA_NOPE], r1, r2, zq]
        qb_cols += [-r2, r1, zq]
    kv = w_kv_b.reshape(MLA_KV_LORA, MLA_H, MLA_NOPE + MLA_V)
    pad = lambda v: jnp.pad(v.astype(F32), (GDN_H, LANE - 2 * GDN_H)).reshape(1, LANE)
    return {
        "w_e1": w_e1,
        "conv_w": conv_w.astype(F32),
        "alog": pad(a_log), "dtb": pad(dt_bias),
        "norm_w": norm_w.reshape(1, -1).astype(F32),
        "q_norm": q_norm.reshape(1, -1).astype(F32),
        "kv_norm": kv_norm.reshape(1, -1).astype(F32),
        "wqa": jnp.concatenate(qa_cols, axis=1).astype(BF16),
        "wqb": jnp.concatenate(qb_cols, axis=1).astype(BF16),
        "wkn": kv[:, :, :MLA_NOPE].reshape(MLA_KV_LORA, -1).astype(BF16),
        "wv": kv[:, :, MLA_NOPE:].reshape(MLA_KV_LORA, -1).astype(BF16),
        "w_out": w_out.astype(BF16),
    }


def _rope_tables(pos):
    half = MLA_ROPE // 2
    inv = ROPE_THETA ** (-jnp.arange(half, dtype=F32) / half)
    ang = pos.astype(F32)[:, None] * inv[None, :]
    z = jnp.zeros((pos.shape[0], LANE - MLA_ROPE), F32)
    cos, sin = jnp.cos(ang), jnp.sin(ang)
    return jnp.concatenate([cos, cos, z], axis=1), jnp.concatenate([sin, sin, z], axis=1)


def _prep_s5(lam_re, lam_im, log_dt, b_re, b_im, c_re, c_im, d_skip, w_glu, w_out):
    n = lam_re.shape[0]
    rep = lambda a: jnp.repeat(a.astype(F32), S5_M, axis=1)
    to_gmp = lambda a: jnp.transpose(a.astype(F32), (0, 1, 3, 2)).reshape(n, D_MODEL, S5_P)
    ab_re, ab_im, bb_re, bb_im = _s5prep_call(
        rep(lam_re), rep(lam_im), jnp.repeat(log_dt.astype(F32), S5_M, axis=1)[..., None],
        to_gmp(b_re), to_gmp(b_im))
    eye = jnp.eye(S5_GPT, dtype=F32)

    def embed_in(bb):
        t = bb.reshape(n, S5_NKT, S5_GPT, S5_M, S5_P)
        return jnp.einsum("nkgmp,gh->nkgmhp", t, eye).reshape(n, S5_NKT, S5_KT, S5_SPT)

    def embed_out(c):
        t = c.astype(F32).reshape(n, S5_NKT, S5_GPT, S5_M, S5_P)
        return jnp.einsum("nkgmp,gh->nkhpgm", t, eye).reshape(n, S5_NKT, S5_SPT, S5_KT)

    pick = lambda a: a.reshape(n, S5_G, S5_M, S5_P)[:, :, 0, :].reshape(n, 1, S5_STATE)
    return {
        "ab_re": pick(ab_re), "ab_im": pick(ab_im),
        "bb": jnp.concatenate([embed_in(bb_re), embed_in(bb_im)], axis=-1).astype(BF16),
        "c_re": embed_out(c_re).astype(BF16), "c_im": embed_out(c_im).astype(BF16),
        "d_skip": d_skip.reshape(n, 1, D_MODEL).astype(F32),
        "w_glu": w_glu.astype(BF16), "w_out": w_out.astype(BF16),
    }


def _pad_rows(a, n):
    return jnp.pad(a, ((0, 0), (0, n - a.shape[1]), (0, 0)))


def _run_trunk(x, off, pos, mod, mod3, ffw, evw, s5w, ln_g, ln_b, gdn0, conv0, s5re0, s5im0, past, cfg):
    nb, T, _ = x.shape
    cos_t, sin_t = _rope_tables(pos)
    new_ckv, new_kr, new_gdn, new_conv, new_re, new_im = [], [], [], [], [], []
    stream = off // nb
    for l in range(DEPTH):
        lng = lambda s: ln_g[l, s].reshape(1, D_MODEL)
        lnb = lambda s: ln_b[l, s].reshape(1, D_MODEL)
        x = _ffn_call(x, mod3, l, 0, off, *ffw[l][0], lng(0), lnb(0), cfg["ffn_bb"], cfg["ffn_tt"])
        i = l // 2
        if l % 2 == 0:
            ew = evw[i]
            conv0p = jnp.pad(conv0[i].astype(F32), ((0, 0), (SUBLANE - (GDN_CONV - 1), 0), (0, 0)))
            qkvn, z, bg, qa, ka, vv, ckv, kr, tail = _e1_call(
                x, mod3, l, off, ew, conv0p, cos_t, sin_t, cfg["e1_bb"], cfg["e1_tt"])
            n_valid = min(T, GDN_C)
            if T < GDN_C:
                qkvn, z, bg = (_pad_rows(a, GDN_C) for a in (qkvn, z, bg))
            og, s_new = _gdn_call(qkvn, z, bg, gdn0[i].astype(F32), ew, cfg["gdn_bb"], n_valid)
            og = og[:, :T]
            if past is None:
                x = _mla_prompt_call(qa, ka, vv, og, x, mod3, l, off, ew["w_out"], lng(1), lnb(1), cfg["mla_tq"])
            else:
                kp, vp = _kvpast_call(past[0][i], past[1][i], ew)
                x = _mla_sample_call(qa, kp, vp, ka, vv, og, x, mod3, l, off, ew["w_out"], lng(1), lnb(1),
                                     cfg["mla_tk"])
            new_gdn.append(s_new)
            new_conv.append(tail[:, SUBLANE - (GDN_CONV - 1):])
            new_ckv.append(ckv)
            new_kr.append(kr)
        else:
            sw = {k: v[i] for k, v in s5w.items()}
            xt = jnp.transpose(x, (1, 0, 2))
            xt, h_re, h_im = _s5_call(xt, mod, l, stream, s5re0[i].reshape(nb, S5_STATE).astype(F32),
                                      s5im0[i].reshape(nb, S5_STATE).astype(F32), sw, lng(1), lnb(1), cfg["s5_tt"])
            x = jnp.transpose(xt, (1, 0, 2))
            new_re.append(h_re.reshape(nb, S5_G, S5_P))
            new_im.append(h_im.reshape(nb, S5_G, S5_P))
        x = _ffn_call(x, mod3, l, 2, off, *ffw[l][1], lng(2), lnb(2), cfg["ffn_bb"], cfg["ffn_tt"])
    return (x, jnp.stack(new_ckv), jnp.stack(new_kr), jnp.stack(new_gdn), jnp.stack(new_conv),
            jnp.stack(new_re), jnp.stack(new_im))


def _config(nb, T):
    if T >= 512:
        return {"ffn_bb": 1, "ffn_tt": 512, "e1_bb": 1, "e1_tt": 512, "gdn_bb": 1, "mla_tq": 256,
                "mla_tk": 256, "s5_tt": 32}
    bb = max(1, min(nb, 512 // T))
    return {"ffn_bb": bb, "ffn_tt": T, "e1_bb": bb, "e1_tt": T, "gdn_bb": 1, "mla_tq": T, "mla_tk": 256,
            "s5_tt": min(T, 32)}


def kernel(x_prompt, x_sample, c_prompt, c_sample, cache_mla_latent, cache_mla_krope, state_gdn, state_gdn_conv, state_s5_re, state_s5_im, w_ada, b_ada, ln_g, ln_b, w_ff_in, w_ff_out, w_mix_in, gdn_conv_w, gdn_a_log, gdn_dt_bias, gdn_norm_w, mla_q_norm, mla_w_q_b, mla_kv_norm, mla_w_kv_b, w_mix_out, s5_lambda_re, s5_lambda_im, s5_log_dt, s5_b_re, s5_b_im, s5_c_re, s5_c_im, s5_d, s5_w_glu, s5_w_out):
    bp, sp, _ = x_prompt.shape
    bs, ss, _ = x_sample.shape
    assert bp == bs, "the modulation table is indexed in blocks of one stream's batch"

    mod = _ada_call(jnp.concatenate([c_prompt, c_sample], axis=0), w_ada, b_ada)
    mod3 = mod.reshape(DEPTH * (bp + bs), 1, -1)

    ffw = [[_prep_ffn(w_ff_in[l, s], w_ff_out[l, s]) for s in range(2)] for l in range(DEPTH)]
    evw = [_prep_even(w_mix_in[i], gdn_conv_w[i], gdn_a_log[i], gdn_dt_bias[i], gdn_norm_w[i], mla_q_norm[i],
                      mla_w_q_b[i], mla_kv_norm[i], mla_w_kv_b[i], w_mix_out[i]) for i in range(N_EVEN)]
    s5w = _prep_s5(s5_lambda_re, s5_lambda_im, s5_log_dt, s5_b_re, s5_b_im, s5_c_re, s5_c_im, s5_d,
                   s5_w_glu, s5_w_out)

    zeros = lambda *s: jnp.zeros(s, F32)
    outs_p = _run_trunk(
        x_prompt, 0, jnp.arange(sp, dtype=jnp.int32), mod, mod3, ffw, evw, s5w, ln_g, ln_b,
        zeros(N_EVEN, bp, GDN_H, GDN_DK, GDN_DV), zeros(N_EVEN, bp, GDN_CONV - 1, GDN_QKV),
        zeros(N_ODD, bp, S5_G, S5_P), zeros(N_ODD, bp, S5_G, S5_P), None, _config(bp, sp))

    n_past = cache_mla_latent.shape[2]
    past_kr = jnp.pad(cache_mla_krope, ((0, 0), (0, 0), (0, 0), (0, LANE - MLA_ROPE)))
    outs_s = _run_trunk(
        x_sample, bp, n_past + jnp.arange(ss, dtype=jnp.int32), mod, mod3, ffw, evw, s5w, ln_g, ln_b,
        state_gdn, state_gdn_conv, state_s5_re, state_s5_im, (cache_mla_latent, past_kr), _config(bs, ss))

    y_p, lat_p, kr_p, gdn_p, conv_p, re_p, im_p = outs_p
    y_s, lat_s, kr_s, gdn_s, conv_s, re_s, im_s = outs_s
    return (y_p, y_s, lat_p, kr_p, gdn_p, conv_p, re_p, im_p, lat_s, kr_s, gdn_s, conv_s, re_s, im_s)
```

```python
import functools
import math

import jax
import jax.numpy as jnp
from jax import lax
from jax.experimental import pallas as pl
from jax.experimental.pallas import tpu as pltpu

F32 = jnp.float32
BF16 = jnp.bfloat16

D_MODEL = 1024
DEPTH = 4
CHUNK = 64
N_EVEN = (DEPTH + 1) // 2
N_ODD = DEPTH // 2
DEEP_ALPHA = (2.0 * DEPTH) ** 0.25
EPS = 1e-6
D_FF = 2816

GDN_H = 4
GDN_DK = 128
GDN_DV = 128
GDN_CONV = 4
GDN_QKV = GDN_H * (2 * GDN_DK + GDN_DV)

MLA_H = 4
MLA_NOPE = 128
MLA_ROPE = 64
MLA_V = 128
MLA_Q_LORA = 256
MLA_KV_LORA = 128
MLA_SCALE = (MLA_NOPE + MLA_ROPE) ** -0.5
ROPE_THETA = 10000.0

OFF_Z = GDN_QKV
OFF_B = OFF_Z + GDN_H * GDN_DV
OFF_A = OFF_B + GDN_H
OFF_QA = OFF_A + GDN_H
OFF_KVA = OFF_QA + MLA_Q_LORA

S5_M = 16
S5_G = D_MODEL // S5_M
S5_P = 64
S5_STATE = S5_G * S5_P
S5_KT = 256
S5_NKT = D_MODEL // S5_KT
S5_GPT = S5_KT // S5_M
S5_SPT = S5_GPT * S5_P

LANE = 128
SUBLANE = 8
VMEM_LIMIT = 56 * 1024 * 1024

FF_CHUNK = 256
FF_NCH = D_FF // FF_CHUNK
GDN_C = 128

E1_QKV = 0
E1_Z = GDN_QKV
E1_QA = E1_Z + GDN_H * GDN_DV
E1_CKV = E1_QA + MLA_Q_LORA
E1_KRA = E1_CKV + MLA_KV_LORA
E1_KRB = E1_KRA + LANE
E1_BA = E1_KRB + LANE
E1_COLS = E1_BA + LANE


def _cparams(sem):
    return pltpu.CompilerParams(dimension_semantics=sem, vmem_limit_bytes=VMEM_LIMIT)


def _dot(a, b):
    return jnp.dot(a, b, preferred_element_type=F32)


def _dot_nt(a, b):
    return lax.dot_general(a, b, (((1,), (1,)), ((), ())), preferred_element_type=F32)


def _sigmoid(x):
    return 1.0 / (1.0 + jnp.exp(-x))


def _silu(x):
    return x * _sigmoid(x)


def _layer_norm(y, g, b):
    mu = jnp.mean(y, axis=-1, keepdims=True)
    yc = y - mu
    var = jnp.mean(yc * yc, axis=-1, keepdims=True)
    return yc * lax.rsqrt(var + EPS) * g + b


def _rms_norm(y, g):
    return y * lax.rsqrt(jnp.mean(y * y, axis=-1, keepdims=True) + EPS) * g


def _split3(a):
    h = a.astype(BF16)
    r = a - h.astype(F32)
    m = r.astype(BF16)
    l = (r - m.astype(F32)).astype(BF16)
    return h, m, l


def _mm3(a, b):
    ah = a.astype(BF16)
    al = (a - ah.astype(F32)).astype(BF16)
    bh = b.astype(BF16)
    bl = (b - bh.astype(F32)).astype(BF16)
    return _dot(ah, bh) + _dot(ah, bl) + _dot(al, bh)


def _ada_kernel(c_ref, w_ref, b_ref, o_ref):
    sc = _silu(c_ref[...]).astype(BF16)
    o_ref[0] = _dot(sc, w_ref[0].astype(BF16)) + b_ref[0]


def _ada_call(c_all, w_ada, b_ada):
    n = c_all.shape[0]
    ncol = w_ada.shape[-1] // D_MODEL
    return pl.pallas_call(
        _ada_kernel,
        grid=(DEPTH, ncol),
        in_specs=[
            pl.BlockSpec((n, D_MODEL), lambda l, j: (0, 0)),
            pl.BlockSpec((1, D_MODEL, D_MODEL), lambda l, j: (l, 0, j)),
            pl.BlockSpec((1, 1, D_MODEL), lambda l, j: (l, 0, j)),
        ],
        out_specs=pl.BlockSpec((1, n, D_MODEL), lambda l, j: (l, 0, j)),
        out_shape=jax.ShapeDtypeStruct((DEPTH, n, ncol * D_MODEL), F32),
        compiler_params=_cparams(("parallel", "parallel")),
        name="ada_mod",
    )(c_all, w_ada, b_ada.reshape(DEPTH, 1, -1))


def _mod_spec(mod3, layer, sub, part, off, bb):
    base = (layer * (mod3.shape[0] // DEPTH) + off) // bb
    col = sub * 3 + part
    return pl.BlockSpec((bb, 1, D_MODEL), lambda i, t: (base + i, 0, col))


def _ffn_kernel(x_ref, sh_ref, sc_ref, gt_ref, win_ref, wout_ref, lng_ref, lnb_ref, o_ref, ub_sc, h_sc):
    bb, tt, _ = x_ref.shape
    for b in range(bb):
        ub_sc[b * tt:(b + 1) * tt] = (x_ref[b] * (1.0 + sc_ref[b]) + sh_ref[b]).astype(BF16)
    for c in range(FF_NCH):
        gu = _dot(ub_sc[...], win_ref[c])
        h_sc[:, c * FF_CHUNK:(c + 1) * FF_CHUNK] = (_silu(gu[:, :FF_CHUNK]) * gu[:, FF_CHUNK:]).astype(BF16)
    h = _dot(h_sc[...], wout_ref[...])
    for b in range(bb):
        y = DEEP_ALPHA * x_ref[b] + 0.5 * (1.0 + gt_ref[b]) * h[b * tt:(b + 1) * tt]
        o_ref[b] = _layer_norm(y, lng_ref[...], lnb_ref[...])


def _ffn_call(x, mod3, layer, sub, off, win_r, wout_r, lng, lnb, bb, tt):
    nb, T, _ = x.shape
    const3 = lambda i, t: (0, 0, 0)
    const2 = lambda i, t: (0, 0)
    rows = bb * tt
    return pl.pallas_call(
        _ffn_kernel,
        grid=(nb // bb, T // tt),
        in_specs=[
            pl.BlockSpec((bb, tt, D_MODEL), lambda i, t: (i, t, 0)),
            _mod_spec(mod3, layer, sub, 0, off, bb),
            _mod_spec(mod3, layer, sub, 1, off, bb),
            _mod_spec(mod3, layer, sub, 2, off, bb),
            pl.BlockSpec(win_r.shape, const3, pipeline_mode=pl.Buffered(1)),
            pl.BlockSpec(wout_r.shape, const2, pipeline_mode=pl.Buffered(1)),
            pl.BlockSpec((1, D_MODEL), const2),
            pl.BlockSpec((1, D_MODEL), const2),
        ],
        out_specs=pl.BlockSpec((bb, tt, D_MODEL), lambda i, t: (i, t, 0)),
        out_shape=jax.ShapeDtypeStruct(x.shape, F32),
        scratch_shapes=[pltpu.VMEM((rows, D_MODEL), BF16), pltpu.VMEM((rows, D_FF), BF16)],
        compiler_params=_cparams(("parallel", "parallel")),
        name="ffn",
    )(x, mod3, mod3, mod3, win_r, wout_r, lng, lnb)


def _e1_kernel(x_ref, sh_ref, sc_ref, w_ref, conv0_ref, convw_ref, cos_ref, sin_ref, cost_ref, sint_ref,
               qn_ref, kvn_ref, wqa_ref, wqb_ref, wkn_ref, wv_ref,
               qkvn_ref, z_ref, bg_ref, qa_ref, ka_ref, vv_ref, ckv_ref, kr_ref, tail_ref, tail_sc,
               *, feature_major):
    bb, tt, _ = x_ref.shape
    rows = bb * tt
    assert bb == 1 or not feature_major

    @pl.when(pl.program_id(1) == 0)
    def _():
        tail_sc[...] = conv0_ref[...]

    u = x_ref[...] * (1.0 + sc_ref[...]) + sh_ref[...]
    ub = u.reshape(rows, D_MODEL).astype(BF16)

    pre = _dot(ub, w_ref[:, E1_QKV:E1_Z])
    for b in range(bb):
        pre_b = pre[b * tt:(b + 1) * tt]
        hist = jnp.concatenate([tail_sc[b], pre_b], axis=0)
        y = pre_b * convw_ref[GDN_CONV - 1:GDN_CONV, :]
        for lag in range(1, GDN_CONV):
            w_row = convw_ref[GDN_CONV - 1 - lag:GDN_CONV - lag, :]
            y = y + pltpu.roll(hist, lag, 0)[SUBLANE:] * w_row
        tail_sc[b] = pre_b[tt - SUBLANE:]
        y = _silu(y)
        for h in range(GDN_H):
            qh = y[:, h * GDN_DK:(h + 1) * GDN_DK]
            qkvn_ref[b, :, h * GDN_DK:(h + 1) * GDN_DK] = (
                qh * lax.rsqrt(jnp.sum(qh * qh, axis=-1, keepdims=True) + EPS) * (GDN_DK ** -0.5))
            ko = GDN_H * GDN_DK + h * GDN_DK
            kh = y[:, ko:ko + GDN_DK]
            qkvn_ref[b, :, ko:ko + GDN_DK] = kh * lax.rsqrt(jnp.sum(kh * kh, axis=-1, keepdims=True) + EPS)
        vo = 2 * GDN_H * GDN_DK
        qkvn_ref[b, :, vo:] = y[:, vo:]
    tail_ref[...] = tail_sc[...]

    z_ref[...] = _dot(ub, w_ref[:, E1_Z:E1_QA]).reshape(bb, tt, -1)
    bg_ref[...] = _dot(ub, w_ref[:, E1_BA:E1_COLS]).reshape(bb, tt, -1)

    cos = cos_ref[...]
    sin = sin_ref[...]
    if bb > 1:
        cos = jnp.concatenate([cos] * bb, axis=0)
        sin = jnp.concatenate([sin] * bb, axis=0)

    cq = _rms_norm(_dot(ub, w_ref[:, E1_QA:E1_CKV]), qn_ref[...])
    if feature_major:
        cq_t = cq.T.astype(BF16)
        qa = _dot(wqa_ref[...], cq_t)
        qb = _dot(wqb_ref[...], cq_t)
        for h in range(MLA_H):
            o = 2 * LANE * h
            qa_ref[0, o:o + LANE, :] = qa[o:o + LANE].astype(BF16)
            rot = qa[o + LANE:o + 2 * LANE] * cost_ref[...] + qb[h * LANE:(h + 1) * LANE] * sint_ref[...]
            qa_ref[0, o + LANE:o + 2 * LANE, :] = rot.astype(BF16)
    else:
        cqb = cq.astype(BF16)
        qa = _dot(cqb, wqa_ref[...])
        qb = _dot(cqb, wqb_ref[...])
        for h in range(MLA_H):
            o = 2 * LANE * h
            qa_ref[:, :, o:o + LANE] = qa[:, o:o + LANE].astype(BF16).reshape(bb, tt, LANE)
            rot = qa[:, o + LANE:o + 2 * LANE] * cos + qb[:, h * LANE:(h + 1) * LANE] * sin
            qa_ref[:, :, o + LANE:o + 2 * LANE] = rot.astype(BF16).reshape(bb, tt, LANE)

    ckv = _rms_norm(_dot(ub, w_ref[:, E1_CKV:E1_KRA]), kvn_ref[...])
    ckv_ref[...] = ckv.reshape(bb, tt, -1)
    kr = _dot(ub, w_ref[:, E1_KRA:E1_KRB]) * cos + _dot(ub, w_ref[:, E1_KRB:E1_BA]) * sin
    kr_ref[...] = kr[:, :MLA_ROPE].reshape(bb, tt, -1)
    cb = ckv.astype(BF16)
    krb = kr.astype(BF16).reshape(bb, tt, LANE)
    kn = _dot(cb, wkn_ref[...])
    for h in range(MLA_H):
        o = 2 * LANE * h
        ka_ref[:, :, o:o + LANE] = kn[:, h * LANE:(h + 1) * LANE].astype(BF16).reshape(bb, tt, LANE)
        ka_ref[:, :, o + LANE:o + 2 * LANE] = krb
    if feature_major:
        vv_ref[0] = _dot(wv_ref[...], ckv.T.astype(BF16)).astype(BF16)
    else:
        vv_ref[...] = _dot(cb, wv_ref[...]).astype(BF16).reshape(bb, tt, -1)


def _e1_call(x, mod3, layer, off, ew, conv0p, rope, bb, tt, feature_major):
    nb, T, _ = x.shape
    tok = lambda i, t: (i, t, 0)
    const2 = lambda i, t: (0, 0)

    def full(a):
        return pl.BlockSpec(a.shape, const2)

    out_dims = [(GDN_QKV, F32), (GDN_H * GDN_DV, F32), (LANE, F32), (MLA_H * 2 * LANE, BF16),
                (MLA_H * 2 * LANE, BF16), (MLA_H * MLA_V, BF16), (MLA_KV_LORA, F32), (MLA_ROPE, F32)]
    out_shape = [jax.ShapeDtypeStruct((nb, T, d), dt) for d, dt in out_dims]
    out_specs = [pl.BlockSpec((bb, tt, d), tok) for d, _ in out_dims]
    wqa, wqb, wv = ew["wqa"], ew["wqb"], ew["wv"]
    if feature_major:
        wqa, wqb, wv = ew["wqa_t"], ew["wqb_t"], ew["wv_t"]
        for idx in (3, 5):
            d = out_dims[idx][0]
            out_shape[idx] = jax.ShapeDtypeStruct((nb, d, T), BF16)
            out_specs[idx] = pl.BlockSpec((bb, d, tt), lambda i, t: (i, 0, t))
    out_shape.append(jax.ShapeDtypeStruct((nb, SUBLANE, GDN_QKV), F32))
    out_specs.append(pl.BlockSpec((bb, SUBLANE, GDN_QKV), lambda i, t: (i, 0, 0)))
    cos_t, sin_t = rope["rows"]
    cos_f, sin_f = rope["cols"]
    return pl.pallas_call(
        functools.partial(_e1_kernel, feature_major=feature_major),
        grid=(nb // bb, T // tt),
        in_specs=[
            pl.BlockSpec((bb, tt, D_MODEL), tok),
            _mod_spec(mod3, layer, 1, 0, off, bb),
            _mod_spec(mod3, layer, 1, 1, off, bb),
            full(ew["w_e1"]),
            pl.BlockSpec((bb, SUBLANE, GDN_QKV), lambda i, t: (i, 0, 0)),
            full(ew["conv_w"]),
            pl.BlockSpec((tt, LANE), lambda i, t: (t, 0)),
            pl.BlockSpec((tt, LANE), lambda i, t: (t, 0)),
            pl.BlockSpec((LANE, tt), lambda i, t: (0, t)),
            pl.BlockSpec((LANE, tt), lambda i, t: (0, t)),
            full(ew["q_norm"]), full(ew["kv_norm"]),
            full(wqa), full(wqb), full(ew["wkn"]), full(wv),
        ],
        out_specs=out_specs,
        out_shape=out_shape,
        scratch_shapes=[pltpu.VMEM((bb, SUBLANE, GDN_QKV), F32)],
        compiler_params=_cparams(("parallel", "arbitrary")),
        name="mix_proj",
    )(x, mod3, mod3, ew["w_e1"], conv0p, ew["conv_w"], cos_t, sin_t, cos_f, sin_f, ew["q_norm"], ew["kv_norm"],
      wqa, wqb, ew["wkn"], wv)


def _gdn_kernel(qkvn_ref, z_ref, bg_ref, s0_ref, alog_ref, dtb_ref, nw_ref, o_ref, s_ref, s_sc, *, n_valid):
    bb = qkvn_ref.shape[0]
    C = GDN_C
    units = [(b, h) for b in range(bb) for h in range(GDN_H)]

    @pl.when(pl.program_id(1) == 0)
    def _():
        s_sc[...] = s0_ref[...]

    row = lax.broadcasted_iota(jnp.int32, (C, C), 0)
    col = lax.broadcasted_iota(jnp.int32, (C, C), 1)
    incl = row >= col
    strict = row > col
    tri = jnp.where(incl, 1.0, 0.0).astype(BF16)
    eye = jnp.where(row == col, 1.0, 0.0)

    beta_all, G_all, G_t = [], [], []
    for b in range(bb):
        bg = bg_ref[b]
        beta_all.append(_sigmoid(bg))
        xs = bg + dtb_ref[...]
        softplus = jnp.maximum(xs, 0.0) + jnp.log(1.0 + jnp.exp(-jnp.abs(xs)))
        g_all = -jnp.exp(alog_ref[...]) * softplus
        if n_valid < C:
            g_all = jnp.where(row < n_valid, g_all, 0.0)
        gh, gm, gl = _split3(g_all)
        G_all.append(_dot(tri, gh) + _dot(tri, gm) + _dot(tri, gl))
        G_t.append(G_all[b].T)

    def load(b, h, part):
        o = (part * GDN_H + h) * GDN_DK
        return qkvn_ref[b, :, o:o + GDN_DK]

    q = [load(b, h, 0) for b, h in units]
    k = [load(b, h, 1) for b, h in units]
    v = [load(b, h, 2) for b, h in units]
    kb = [x.astype(BF16) for x in k]
    beta = [beta_all[b][:, h:h + 1] for b, h in units]
    Gc = [G_all[b][:, GDN_H + h:GDN_H + h + 1] for b, h in units]
    Gr = [G_t[b][GDN_H + h:GDN_H + h + 1, :] for b, h in units]
    decay = [jnp.where(incl, jnp.exp(jnp.where(incl, c - r, 0.0)), 0.0) for c, r in zip(Gc, Gr)]
    kk = [_dot_nt(x, x) for x in kb]
    lower = [jnp.where(strict, bt * d * m, 0.0) for bt, d, m in zip(beta, decay, kk)]
    inv = [eye - m for m in lower]
    pw = [_mm3(m, m) for m in lower]
    n_sq = int(math.log2(C)) - 1
    for it in range(n_sq):
        inv = [a + _mm3(a, p) for a, p in zip(inv, pw)]
        if it + 1 < n_sq:
            pw = [_mm3(p, p) for p in pw]
    gam = [jnp.exp(c) for c in Gc]
    sol_v = [_mm3(a, bt * x) for a, bt, x in zip(inv, beta, v)]
    sol_k = [_mm3(a, (bt * g) * x).astype(BF16) for a, bt, g, x in zip(inv, beta, gam, k)]
    attn = [(_dot_nt(x.astype(BF16), y) * d).astype(BF16) for x, y, d in zip(q, kb, decay)]
    qg = [(g * x).astype(BF16) for g, x in zip(gam, q)]
    g_end = [c[C - 1:C, :] for c in Gc]
    kd = [(jnp.exp(e - c) * x).T.astype(BF16) for e, c, x in zip(g_end, Gc, k)]

    s0 = [s_sc[b, h] for b, h in units]
    s0b = [x.astype(BF16) for x in s0]
    ub = [(a - _dot(w, s)).astype(BF16) for a, w, s in zip(sol_v, sol_k, s0b)]
    o = [_dot(x, s) + _dot(a, y) for x, s, a, y in zip(qg, s0b, attn, ub)]
    s1 = [jnp.exp(e) * s + _dot(x, y) for e, s, x, y in zip(g_end, s0, kd, ub)]
    for i, (b, h) in enumerate(units):
        s_sc[b, h] = s1[i]
        zh = z_ref[b, :, h * GDN_DV:(h + 1) * GDN_DV]
        o_ref[b, :, h * GDN_DV:(h + 1) * GDN_DV] = (_rms_norm(o[i], nw_ref[...]) * _silu(zh)).astype(BF16)
    s_ref[...] = s_sc[...]


def _gdn_call(qkvn, z, bg, s0, ew, bb, n_valid):
    nb, T, _ = qkvn.shape
    tok = lambda i, c: (i, c, 0)
    const2 = lambda i, c: (0, 0)
    st = lambda i, c: (i, 0, 0, 0)
    return pl.pallas_call(
        functools.partial(_gdn_kernel, n_valid=n_valid),
        grid=(nb // bb, T // GDN_C),
        in_specs=[
            pl.BlockSpec((bb, GDN_C, GDN_QKV), tok),
            pl.BlockSpec((bb, GDN_C, GDN_H * GDN_DV), tok),
            pl.BlockSpec((bb, GDN_C, LANE), tok),
            pl.BlockSpec((bb, GDN_H, GDN_DK, GDN_DV), st),
            pl.BlockSpec((1, LANE), const2),
            pl.BlockSpec((1, LANE), const2),
            pl.BlockSpec((1, GDN_DV), const2),
        ],
        out_specs=[pl.BlockSpec((bb, GDN_C, GDN_H * GDN_DV), tok),
                   pl.BlockSpec((bb, GDN_H, GDN_DK, GDN_DV), st)],
        out_shape=[jax.ShapeDtypeStruct((nb, T, GDN_H * GDN_DV), BF16),
                   jax.ShapeDtypeStruct((nb, GDN_H, GDN_DK, GDN_DV), F32)],
        scratch_shapes=[pltpu.VMEM((bb, GDN_H, GDN_DK, GDN_DV), F32)],
        compiler_params=_cparams(("parallel", "arbitrary")),
        name="gdn",
    )(qkvn, z, bg, s0, ew["alog"], ew["dtb"], ew["norm_w"])


def _attn_update(h, q, k, v, vis, m_sc, l_sc, acc_sc):
    s = _dot_nt(q, k) * MLA_SCALE
    if vis is not None:
        s = jnp.where(vis, s, -jnp.inf)
    m_prev = m_sc[h]
    m_new = jnp.maximum(m_prev, jnp.max(s, axis=-1, keepdims=True))
    alpha = jnp.exp(m_prev - m_new)
    p = jnp.exp(s - m_new)
    l_sc[h] = alpha * l_sc[h] + jnp.sum(p, axis=-1, keepdims=True)
    acc_sc[h] = alpha * acc_sc[h] + _dot(p.astype(BF16), v)
    m_sc[h] = m_new


def _attn_init(m_sc, l_sc, acc_sc):
    m_sc[...] = jnp.full_like(m_sc, -jnp.inf)
    l_sc[...] = jnp.zeros_like(l_sc)
    acc_sc[...] = jnp.zeros_like(acc_sc)


def _mix_out(og, heads_out, x_ref, gt_ref, wout_ref, lng_ref, lnb_ref, o_ref):
    mix = _dot(jnp.concatenate([og] + heads_out, axis=-1), wout_ref[...])
    y = DEEP_ALPHA * x_ref[0] + (1.0 + gt_ref[0]) * mix
    o_ref[0] = _layer_norm(y, lng_ref[...], lnb_ref[...])


def _mla_prompt_kernel(qa_ref, ka_ref, vv_ref, og_ref, x_ref, gt_ref, wout_ref, lng_ref, lnb_ref, o_ref,
                       m_sc, l_sc, acc_sc):
    tq = qa_ref.shape[2]
    i = pl.program_id(1)
    _attn_init(m_sc, l_sc, acc_sc)
    heads = range(MLA_H)
    vis = (lax.broadcasted_iota(jnp.int32, (tq, tq), 1) // CHUNK
           >= lax.broadcasted_iota(jnp.int32, (tq, tq), 0) // CHUNK)

    def block(j, masked):
        ks = pl.multiple_of(j * tq, tq)
        s = [_dot(ka_ref[0, pl.ds(ks, tq), 2 * LANE * h:2 * LANE * (h + 1)],
                  qa_ref[0, 2 * LANE * h:2 * LANE * (h + 1), :]) * MLA_SCALE for h in heads]
        if masked:
            s = [jnp.where(vis, x, -jnp.inf) for x in s]
        m_prev = [m_sc[h] for h in heads]
        m_new = [jnp.maximum(m, jnp.max(x, axis=0, keepdims=True)) for m, x in zip(m_prev, s)]
        p = [jnp.exp(x - m) for x, m in zip(s, m_new)]
        alpha = [jnp.exp(a - b) for a, b in zip(m_prev, m_new)]
        pv = [_dot(vv_ref[0, MLA_V * h:MLA_V * (h + 1), pl.ds(ks, tq)], p[h].astype(BF16)) for h in heads]
        for h in heads:
            l_sc[h] = alpha[h] * l_sc[h] + jnp.sum(p[h], axis=0, keepdims=True)
            acc_sc[h] = alpha[h] * acc_sc[h] + pv[h]
            m_sc[h] = m_new[h]

    def body(j, carry):
        block(j, False)
        return carry

    lax.fori_loop(0, i, body, 0)
    block(i, True)
    heads_out = [(acc_sc[h] / l_sc[h]).T.astype(BF16) for h in heads]
    _mix_out(og_ref[0], heads_out, x_ref, gt_ref, wout_ref, lng_ref, lnb_ref, o_ref)


def _mla_sample_kernel(qa_ref, kp_ref, vp_ref, ka_ref, vv_ref, og_ref, x_ref, gt_ref, wout_ref, lng_ref, lnb_ref,
                       o_ref, m_sc, l_sc, acc_sc, *, tk):
    _attn_init(m_sc, l_sc, acc_sc)
    n_past = kp_ref.shape[1] // tk
    for h in range(MLA_H):
        q = qa_ref[0, :, 2 * LANE * h:2 * LANE * (h + 1)]
        for j in range(n_past):
            _attn_update(h, q, kp_ref[0, j * tk:(j + 1) * tk, 2 * LANE * h:2 * LANE * (h + 1)],
                         vp_ref[0, j * tk:(j + 1) * tk, MLA_V * h:MLA_V * (h + 1)], None, m_sc, l_sc, acc_sc)
        _attn_update(h, q, ka_ref[0, :, 2 * LANE * h:2 * LANE * (h + 1)],
                     vv_ref[0, :, MLA_V * h:MLA_V * (h + 1)], None, m_sc, l_sc, acc_sc)
    heads_out = [(acc_sc[h] / l_sc[h]).astype(BF16) for h in range(MLA_H)]
    _mix_out(og_ref[0], heads_out, x_ref, gt_ref, wout_ref, lng_ref, lnb_ref, o_ref)


def _mla_scratch(tq, transposed):
    if transposed:
        return [pltpu.VMEM((MLA_H, 1, tq), F32), pltpu.VMEM((MLA_H, 1, tq), F32),
                pltpu.VMEM((MLA_H, MLA_V, tq), F32)]
    return [pltpu.VMEM((MLA_H, tq, 1), F32), pltpu.VMEM((MLA_H, tq, 1), F32), pltpu.VMEM((MLA_H, tq, MLA_V), F32)]


def _mla_prompt_call(qa, ka, vv, og, x, mod3, layer, off, w_out, lng, lnb, tq):
    nb, T, _ = x.shape
    tok = lambda b, i: (b, i, 0)
    seq = lambda b, i: (b, 0, 0)
    const2 = lambda b, i: (0, 0)
    return pl.pallas_call(
        _mla_prompt_kernel,
        grid=(nb, T // tq),
        in_specs=[
            pl.BlockSpec((1, qa.shape[1], tq), lambda b, i: (b, 0, i)),
            pl.BlockSpec((1,) + ka.shape[1:], seq),
            pl.BlockSpec((1,) + vv.shape[1:], seq),
            pl.BlockSpec((1, tq, og.shape[-1]), tok),
            pl.BlockSpec((1, tq, D_MODEL), tok),
            _mod_spec(mod3, layer, 1, 2, off, 1),
            pl.BlockSpec(w_out.shape, const2),
            pl.BlockSpec((1, D_MODEL), const2),
            pl.BlockSpec((1, D_MODEL), const2),
        ],
        out_specs=pl.BlockSpec((1, tq, D_MODEL), tok),
        out_shape=jax.ShapeDtypeStruct(x.shape, F32),
        scratch_shapes=_mla_scratch(tq, True),
        compiler_params=_cparams(("parallel", "arbitrary")),
        name="mla_prompt",
    )(qa, ka, vv, og, x, mod3, w_out, lng, lnb)


def _mla_sample_call(qa, kp, vp, ka, vv, og, x, mod3, layer, off, w_out, lng, lnb, tk):
    nb, T, _ = x.shape
    seq = lambda b, i: (b, 0, 0)
    const2 = lambda b, i: (0, 0)

    def whole(a):
        return pl.BlockSpec((1,) + a.shape[1:], seq)

    return pl.pallas_call(
        functools.partial(_mla_sample_kernel, tk=tk),
        grid=(nb, 1),
        in_specs=[whole(qa), whole(kp), whole(vp), whole(ka), whole(vv), whole(og), whole(x),
                  _mod_spec(mod3, layer, 1, 2, off, 1),
                  pl.BlockSpec(w_out.shape, const2),
                  pl.BlockSpec((1, D_MODEL), const2),
                  pl.BlockSpec((1, D_MODEL), const2)],
        out_specs=whole(x),
        out_shape=jax.ShapeDtypeStruct(x.shape, F32),
        scratch_shapes=_mla_scratch(T, False),
        compiler_params=_cparams(("parallel", "arbitrary")),
        name="mla_sample",
    )(qa, kp, vp, ka, vv, og, x, mod3, w_out, lng, lnb)


def _kvpast_kernel(ckv_ref, kr_ref, wkn_ref, wv_ref, ka_ref, vv_ref):
    cb = ckv_ref[0].astype(BF16)
    kn = _dot(cb, wkn_ref[...])
    krb = kr_ref[0].astype(BF16)
    for h in range(MLA_H):
        o = 2 * LANE * h
        ka_ref[0, :, o:o + LANE] = kn[:, h * LANE:(h + 1) * LANE].astype(BF16)
        ka_ref[0, :, o + LANE:o + 2 * LANE] = krb
    vv_ref[0] = _dot(cb, wv_ref[...]).astype(BF16)


def _kvpast_call(past_ckv, past_kr128, ew):
    nb, P, _ = past_ckv.shape
    seq = lambda b: (b, 0, 0)
    const2 = lambda b: (0, 0)
    return pl.pallas_call(
        _kvpast_kernel,
        grid=(nb,),
        in_specs=[pl.BlockSpec((1, P, MLA_KV_LORA), seq), pl.BlockSpec((1, P, LANE), seq),
                  pl.BlockSpec(ew["wkn"].shape, const2), pl.BlockSpec(ew["wv"].shape, const2)],
        out_specs=[pl.BlockSpec((1, P, MLA_H * 2 * LANE), seq), pl.BlockSpec((1, P, MLA_H * MLA_V), seq)],
        out_shape=[jax.ShapeDtypeStruct((nb, P, MLA_H * 2 * LANE), BF16),
                   jax.ShapeDtypeStruct((nb, P, MLA_H * MLA_V), BF16)],
        compiler_params=_cparams(("parallel",)),
        name="mla_past_kv",
    )(past_ckv, past_kr128, ew["wkn"], ew["wv"])


def _s5prep_kernel(lre_ref, lim_ref, ldt_ref, br_ref, bi_ref, abre_ref, abim_ref, bbre_ref, bbim_ref):
    dt = jnp.exp(ldt_ref[0])
    lr = jnp.minimum(lre_ref[0], -1e-4)
    li = lim_ref[0]
    mag = jnp.exp(lr * dt)
    ab_re = mag * jnp.cos(li * dt)
    ab_im = mag * jnp.sin(li * dt)
    den = lr * lr + li * li
    coef_re = ((ab_re - 1.0) * lr + ab_im * li) / den
    coef_im = (ab_im * lr - (ab_re - 1.0) * li) / den
    br = br_ref[0]
    bi = bi_ref[0]
    abre_ref[0] = ab_re
    abim_ref[0] = ab_im
    bbre_ref[0] = coef_re * br - coef_im * bi
    bbim_ref[0] = coef_re * bi + coef_im * br


def _s5prep_call(lre_x, lim_x, ldt_x, br_t, bi_t):
    n = lre_x.shape[0]
    blk = pl.BlockSpec((1, D_MODEL, S5_P), lambda i: (i, 0, 0))
    return pl.pallas_call(
        _s5prep_kernel,
        grid=(n,),
        in_specs=[blk, blk, pl.BlockSpec((1, D_MODEL, 1), lambda i: (i, 0, 0)), blk, blk],
        out_specs=[blk] * 4,
        out_shape=[jax.ShapeDtypeStruct((n, D_MODEL, S5_P), F32)] * 4,
        compiler_params=_cparams(("parallel",)),
        name="s5_discretise",
    )(lre_x, lim_x, ldt_x, br_t, bi_t)


def _s5_kernel(x_ref, sh_ref, sc_ref, gt_ref, h0re_ref, h0im_ref, abre_ref, abim_ref, bb_ref, cre_ref, cim_ref,
               dskip_ref, wglu_ref, wout_ref, lng_ref, lnb_ref, o_ref, hre_ref, him_ref,
               hre_sc, him_sc, st_sc, y_sc):
    tt, nb, _ = x_ref.shape
    rows = tt * nb
    half = S5_SPT // 2

    @pl.when(pl.program_id(0) == 0)
    def _():
        hre_sc[...] = h0re_ref[...]
        him_sc[...] = h0im_ref[...]

    x = x_ref[...]
    u = (x * (1.0 + sc_ref[...]) + sh_ref[...]).reshape(rows, D_MODEL)
    ub = u.astype(BF16)
    for kt in range(S5_NKT):
        st_sc[...] = _dot(ub[:, kt * S5_KT:(kt + 1) * S5_KT], bb_ref[kt])
        for lc in range(2):
            so = kt * S5_SPT + lc * half
            a_re = jnp.broadcast_to(abre_ref[:, so:so + half], (nb, half))
            a_im = jnp.broadcast_to(abim_ref[:, so:so + half], (nb, half))

            def step(t, carry):
                h_re, h_im = carry
                r0 = pl.multiple_of(t * nb, nb)
                re_sl = (pl.ds(r0, nb), pl.ds(lc * half, half))
                im_sl = (pl.ds(r0, nb), pl.ds(S5_SPT + lc * half, half))
                n_re = a_re * h_re - a_im * h_im + st_sc[re_sl]
                n_im = a_re * h_im + a_im * h_re + st_sc[im_sl]
                st_sc[re_sl] = n_re
                st_sc[im_sl] = n_im
                return n_re, n_im

            h_re, h_im = lax.fori_loop(0, tt, step, (hre_sc[:, so:so + half], him_sc[:, so:so + half]))
            hre_sc[:, so:so + half] = h_re
            him_sc[:, so:so + half] = h_im
        y_sc[:, kt * S5_KT:(kt + 1) * S5_KT] = (
            _dot(st_sc[:, :S5_SPT].astype(BF16), cre_ref[kt]) - _dot(st_sc[:, S5_SPT:].astype(BF16), cim_ref[kt]))

    y = y_sc[...] + dskip_ref[...] * u
    yg = 0.5 * y * (1.0 + jnp.tanh(math.sqrt(2.0 / math.pi) * (y + 0.044715 * (y * y * y))))
    glu = _dot(yg.astype(BF16), wglu_ref[...])
    mix = _dot((yg * _sigmoid(glu)).astype(BF16), wout_ref[...])
    out = DEEP_ALPHA * x + (1.0 + gt_ref[...]) * mix.reshape(tt, nb, D_MODEL)
    o_ref[...] = _layer_norm(out, lng_ref[...], lnb_ref[...])

    @pl.when(pl.program_id(0) == pl.num_programs(0) - 1)
    def _():
        hre_ref[...] = hre_sc[...]
        him_ref[...] = him_sc[...]


def _s5_call(xt, mod, layer, stream, h0re, h0im, sw, lng, lnb, tt):
    T, nb, _ = xt.shape
    rows = tt * nb
    const2 = lambda t: (0, 0)
    const3 = lambda t: (0, 0, 0)

    def mspec(part):
        return pl.BlockSpec((1, nb, D_MODEL), lambda t: (layer, stream, 3 + part))

    def full(a):
        return pl.BlockSpec(a.shape, const2 if a.ndim == 2 else const3)

    return pl.pallas_call(
        _s5_kernel,
        grid=(T // tt,),
        in_specs=[
            pl.BlockSpec((tt, nb, D_MODEL), lambda t: (t, 0, 0)),
            mspec(0), mspec(1), mspec(2),
            full(h0re), full(h0im), full(sw["ab_re"]), full(sw["ab_im"]),
            full(sw["bb"]), full(sw["c_re"]), full(sw["c_im"]), full(sw["d_skip"]),
            full(sw["w_glu"]), full(sw["w_out"]), full(lng), full(lnb),
        ],
        out_specs=[pl.BlockSpec((tt, nb, D_MODEL), lambda t: (t, 0, 0)),
                   pl.BlockSpec((nb, S5_STATE), const2), pl.BlockSpec((nb, S5_STATE), const2)],
        out_shape=[jax.ShapeDtypeStruct(xt.shape, F32),
                   jax.ShapeDtypeStruct((nb, S5_STATE), F32), jax.ShapeDtypeStruct((nb, S5_STATE), F32)],
        scratch_shapes=[pltpu.VMEM((nb, S5_STATE), F32), pltpu.VMEM((nb, S5_STATE), F32),
                        pltpu.VMEM((rows, 2 * S5_SPT), F32), pltpu.VMEM((rows, D_MODEL), F32)],
        compiler_params=_cparams(("arbitrary",)),
        name="s5_mixer",
    )(xt, mod, mod, mod, h0re, h0im, sw["ab_re"], sw["ab_im"], sw["bb"], sw["c_re"], sw["c_im"],
      sw["d_skip"], sw["w_glu"], sw["w_out"], lng, lnb)


def _prep_ffn(w_in, w_out):
    gate = w_in[:, :D_FF].reshape(D_MODEL, FF_NCH, FF_CHUNK)
    up = w_in[:, D_FF:].reshape(D_MODEL, FF_NCH, FF_CHUNK)
    win_r = jnp.transpose(jnp.concatenate([gate, up], axis=-1), (1, 0, 2)).astype(BF16)
    return win_r, w_out.astype(BF16)


def _prep_even(w_in, conv_w, a_log, dt_bias, norm_w, q_norm, w_q_b, kv_norm, w_kv_b, w_out):
    half = MLA_ROPE // 2
    zcol = lambda n: jnp.zeros((w_in.shape[0], n), w_in.dtype)
    k1 = w_in[:, OFF_KVA + MLA_KV_LORA:OFF_KVA + MLA_KV_LORA + half]
    k2 = w_in[:, OFF_KVA + MLA_KV_LORA + half:OFF_KVA + MLA_KV_LORA + MLA_ROPE]
    w_e1 = jnp.concatenate([
        w_in[:, :OFF_B],
        w_in[:, OFF_QA:OFF_KVA + MLA_KV_LORA],
        k1, k2, zcol(LANE - MLA_ROPE),
        -k2, k1, zcol(LANE - MLA_ROPE),
        w_in[:, OFF_B:OFF_QA], zcol(LANE - 2 * GDN_H),
    ], axis=1).astype(BF16)
    zq = jnp.zeros((MLA_Q_LORA, LANE - MLA_ROPE), w_q_b.dtype)
    qa_cols, qb_cols = [], []
    for h in range(MLA_H):
        o = h * (MLA_NOPE + MLA_ROPE)
        r1 = w_q_b[:, o + MLA_NOPE:o + MLA_NOPE + half]
        r2 = w_q_b[:, o + MLA_NOPE + half:o + MLA_NOPE + MLA_ROPE]
        qa_cols += [w_q_b[:, o:o + MLA_NOPE], r1, r2, zq]
        qb_cols += [-r2, r1, zq]
    kv = w_kv_b.reshape(MLA_KV_LORA, MLA_H, MLA_NOPE + MLA_V)
    pad = lambda v: jnp.pad(v.astype(F32), (GDN_H, LANE - 2 * GDN_H)).reshape(1, LANE)
    return {
        "w_e1": w_e1,
        "conv_w": conv_w.astype(F32),
        "alog": pad(a_log), "dtb": pad(dt_bias),
        "norm_w": norm_w.reshape(1, -1).astype(F32),
        "q_norm": q_norm.reshape(1, -1).astype(F32),
        "kv_norm": kv_norm.reshape(1, -1).astype(F32),
        "wqa": jnp.concatenate(qa_cols, axis=1).astype(BF16),
        "wqb": jnp.concatenate(qb_cols, axis=1).astype(BF16),
        "wqa_t": jnp.concatenate(qa_cols, axis=1).T.astype(BF16),
        "wqb_t": jnp.concatenate(qb_cols, axis=1).T.astype(BF16),
        "wv_t": kv[:, :, MLA_NOPE:].reshape(MLA_KV_LORA, -1).T.astype(BF16),
        "wkn": kv[:, :, :MLA_NOPE].reshape(MLA_KV_LORA, -1).astype(BF16),
        "wv": kv[:, :, MLA_NOPE:].reshape(MLA_KV_LORA, -1).astype(BF16),
        "w_out": w_out.astype(BF16),
    }


def _rope_tables(pos):
    half = MLA_ROPE // 2
    inv = ROPE_THETA ** (-jnp.arange(half, dtype=F32) / half)
    ang = pos.astype(F32)[:, None] * inv[None, :]
    z = jnp.zeros((pos.shape[0], LANE - MLA_ROPE), F32)
    cos, sin = jnp.cos(ang), jnp.sin(ang)
    cos, sin = jnp.concatenate([cos, cos, z], axis=1), jnp.concatenate([sin, sin, z], axis=1)
    return {"rows": (cos, sin), "cols": (cos.T, sin.T)}


def _prep_s5(lam_re, lam_im, log_dt, b_re, b_im, c_re, c_im, d_skip, w_glu, w_out):
    n = lam_re.shape[0]
    rep = lambda a: jnp.repeat(a.astype(F32), S5_M, axis=1)
    to_gmp = lambda a: jnp.transpose(a.astype(F32), (0, 1, 3, 2)).reshape(n, D_MODEL, S5_P)
    ab_re, ab_im, bb_re, bb_im = _s5prep_call(
        rep(lam_re), rep(lam_im), jnp.repeat(log_dt.astype(F32), S5_M, axis=1)[..., None],
        to_gmp(b_re), to_gmp(b_im))
    eye = jnp.eye(S5_GPT, dtype=F32)

    def embed_in(bb):
        t = bb.reshape(n, S5_NKT, S5_GPT, S5_M, S5_P)
        return jnp.einsum("nkgmp,gh->nkgmhp", t, eye).reshape(n, S5_NKT, S5_KT, S5_SPT)

    def embed_out(c):
        t = c.astype(F32).reshape(n, S5_NKT, S5_GPT, S5_M, S5_P)
        return jnp.einsum("nkgmp,gh->nkhpgm", t, eye).reshape(n, S5_NKT, S5_SPT, S5_KT)

    pick = lambda a: a.reshape(n, S5_G, S5_M, S5_P)[:, :, 0, :].reshape(n, 1, S5_STATE)
    return {
        "ab_re": pick(ab_re), "ab_im": pick(ab_im),
        "bb": jnp.concatenate([embed_in(bb_re), embed_in(bb_im)], axis=-1).astype(BF16),
        "c_re": embed_out(c_re).astype(BF16), "c_im": embed_out(c_im).astype(BF16),
        "d_skip": d_skip.reshape(n, 1, D_MODEL).astype(F32),
        "w_glu": w_glu.astype(BF16), "w_out": w_out.astype(BF16),
    }


def _pad_rows(a, n):
    return jnp.pad(a, ((0, 0), (0, n - a.shape[1]), (0, 0)))


def _run_trunk(x, off, pos, mod, mod3, ffw, evw, s5w, ln_g, ln_b, gdn0, conv0, s5re0, s5im0, past, cfg):
    nb, T, _ = x.shape
    rope = _rope_tables(pos)
    new_ckv, new_kr, new_gdn, new_conv, new_re, new_im = [], [], [], [], [], []
    stream = off // nb
    for l in range(DEPTH):
        lng = lambda s: ln_g[l, s].reshape(1, D_MODEL)
        lnb = lambda s: ln_b[l, s].reshape(1, D_MODEL)
        x = _ffn_call(x, mod3, l, 0, off, *ffw[l][0], lng(0), lnb(0), cfg["ffn_bb"], cfg["ffn_tt"])
        i = l // 2
        if l % 2 == 0:
            ew = evw[i]
            conv0p = jnp.pad(conv0[i].astype(F32), ((0, 0), (SUBLANE - (GDN_CONV - 1), 0), (0, 0)))
            qkvn, z, bg, qa, ka, vv, ckv, kr, tail = _e1_call(
                x, mod3, l, off, ew, conv0p, rope, cfg["e1_bb"], cfg["e1_tt"], past is None)
            n_valid = min(T, GDN_C)
            if T < GDN_C:
                qkvn, z, bg = (_pad_rows(a, GDN_C) for a in (qkvn, z, bg))
            og, s_new = _gdn_call(qkvn, z, bg, gdn0[i].astype(F32), ew, cfg["gdn_bb"], n_valid)
            og = og[:, :T]
            if past is None:
                x = _mla_prompt_call(qa, ka, vv, og, x, mod3, l, off, ew["w_out"], lng(1), lnb(1), cfg["mla_tq"])
            else:
                kp, vp = _kvpast_call(past[0][i], past[1][i], ew)
                x = _mla_sample_call(qa, kp, vp, ka, vv, og, x, mod3, l, off, ew["w_out"], lng(1), lnb(1),
                                     cfg["mla_tk"])
            new_gdn.append(s_new)
            new_conv.append(tail[:, SUBLANE - (GDN_CONV - 1):])
            new_ckv.append(ckv)
            new_kr.append(kr)
        else:
            sw = {k: v[i] for k, v in s5w.items()}
            xt = jnp.transpose(x, (1, 0, 2))
            xt, h_re, h_im = _s5_call(xt, mod, l, stream, s5re0[i].reshape(nb, S5_STATE).astype(F32),
                                      s5im0[i].reshape(nb, S5_STATE).astype(F32), sw, lng(1), lnb(1), cfg["s5_tt"])
            x = jnp.transpose(xt, (1, 0, 2))
            new_re.append(h_re.reshape(nb, S5_G, S5_P))
            new_im.append(h_im.reshape(nb, S5_G, S5_P))
        x = _ffn_call(x, mod3, l, 2, off, *ffw[l][1], lng(2), lnb(2), cfg["ffn_bb"], cfg["ffn_tt"])
    return (x, jnp.stack(new_ckv), jnp.stack(new_kr), jnp.stack(new_gdn), jnp.stack(new_conv),
            jnp.stack(new_re), jnp.stack(new_im))


def _config(nb, T):
    if T >= 512:
        return {"ffn_bb": 1, "ffn_tt": 512, "e1_bb": 1, "e1_tt": 512, "gdn_bb": 2, "mla_tq": 256,
                "mla_tk": 256, "s5_tt": 32}
    bb = max(1, min(nb, 512 // T))
    return {"ffn_bb": bb, "ffn_tt": T, "e1_bb": bb, "e1_tt": T, "gdn_bb": 2, "mla_tq": T, "mla_tk": 256,
            "s5_tt": min(T, 32)}


def kernel(x_prompt, x_sample, c_prompt, c_sample, cache_mla_latent, cache_mla_krope, state_gdn, state_gdn_conv, state_s5_re, state_s5_im, w_ada, b_ada, ln_g, ln_b, w_ff_in, w_ff_out, w_mix_in, gdn_conv_w, gdn_a_log, gdn_dt_bias, gdn_norm_w, mla_q_norm, mla_w_q_b, mla_kv_norm, mla_w_kv_b, w_mix_out, s5_lambda_re, s5_lambda_im, s5_log_dt, s5_b_re, s5_b_im, s5_c_re, s5_c_im, s5_d, s5_w_glu, s5_w_out):
    bp, sp, _ = x_prompt.shape
    bs, ss, _ = x_sample.shape
    assert bp == bs, "the modulation table is indexed in blocks of one stream's batch"

    mod = _ada_call(jnp.concatenate([c_prompt, c_sample], axis=0), w_ada, b_ada)
    mod3 = mod.reshape(DEPTH * (bp + bs), 1, -1)

    ffw = [[_prep_ffn(w_ff_in[l, s], w_ff_out[l, s]) for s in range(2)] for l in range(DEPTH)]
    evw = [_prep_even(w_mix_in[i], gdn_conv_w[i], gdn_a_log[i], gdn_dt_bias[i], gdn_norm_w[i], mla_q_norm[i],
                      mla_w_q_b[i], mla_kv_norm[i], mla_w_kv_b[i], w_mix_out[i]) for i in range(N_EVEN)]
    s5w = _prep_s5(s5_lambda_re, s5_lambda_im, s5_log_dt, s5_b_re, s5_b_im, s5_c_re, s5_c_im, s5_d,
                   s5_w_glu, s5_w_out)

    zeros = lambda *s: jnp.zeros(s, F32)
    outs_p = _run_trunk(
        x_prompt, 0, jnp.arange(sp, dtype=jnp.int32), mod, mod3, ffw, evw, s5w, ln_g, ln_b,
        zeros(N_EVEN, bp, GDN_H, GDN_DK, GDN_DV), zeros(N_EVEN, bp, GDN_CONV - 1, GDN_QKV),
        zeros(N_ODD, bp, S5_G, S5_P), zeros(N_ODD, bp, S5_G, S5_P), None, _config(bp, sp))

    n_past = cache_mla_latent.shape[2]
    past_kr = jnp.pad(cache_mla_krope, ((0, 0), (0, 0), (0, 0), (0, LANE - MLA_ROPE)))
    outs_s = _run_trunk(
        x_sample, bp, n_past + jnp.arange(ss, dtype=jnp.int32), mod, mod3, ffw, evw, s5w, ln_g, ln_b,
        state_gdn, state_gdn_conv, state_s5_re, state_s5_im, (cache_mla_latent, past_kr), _config(bs, ss))

    y_p, lat_p, kr_p, gdn_p, conv_p, re_p, im_p = outs_p
    y_s, lat_s, kr_s, gdn_s, conv_s, re_s, im_s = outs_s
    return (y_p, y_s, lat_p, kr_p, gdn_p, conv_p, re_p, im_p, lat_s, kr_s, gdn_s, conv_s, re_s, im_s)
```

```python
import functools
import math

import jax
import jax.numpy as jnp
from jax import lax
from jax.experimental import pallas as pl
from jax.experimental.pallas import tpu as pltpu

F32 = jnp.float32
BF16 = jnp.bfloat16

D_MODEL = 1024
DEPTH = 4
CHUNK = 64
N_EVEN = (DEPTH + 1) // 2
N_ODD = DEPTH // 2
DEEP_ALPHA = (2.0 * DEPTH) ** 0.25
EPS = 1e-6
D_FF = 2816

GDN_H = 4
GDN_DK = 128
GDN_DV = 128
GDN_CONV = 4
GDN_QKV = GDN_H * (2 * GDN_DK + GDN_DV)

MLA_H = 4
MLA_NOPE = 128
MLA_ROPE = 64
MLA_V = 128
MLA_Q_LORA = 256
MLA_KV_LORA = 128
MLA_SCALE = (MLA_NOPE + MLA_ROPE) ** -0.5
ROPE_THETA = 10000.0
LOG2_E = math.log2(math.e)

OFF_Z = GDN_QKV
OFF_B = OFF_Z + GDN_H * GDN_DV
OFF_A = OFF_B + GDN_H
OFF_QA = OFF_A + GDN_H
OFF_KVA = OFF_QA + MLA_Q_LORA

S5_M = 16
S5_G = D_MODEL // S5_M
S5_P = 64
S5_STATE = S5_G * S5_P
S5_KT = 256
S5_NKT = D_MODEL // S5_KT
S5_GPT = S5_KT // S5_M
S5_SPT = S5_GPT * S5_P

LANE = 128
SUBLANE = 8
VMEM_LIMIT = 56 * 1024 * 1024

FF_CHUNK = 256
FF_NCH = D_FF // FF_CHUNK
FF_SUB = 512
GDN_C = 128

E1_QKV = 0
E1_Z = GDN_QKV
E1_QA = E1_Z + GDN_H * GDN_DV
E1_CKV = E1_QA + MLA_Q_LORA
E1_KRA = E1_CKV + MLA_KV_LORA
E1_KRB = E1_KRA + LANE
E1_BA = E1_KRB + LANE
E1_COLS = E1_BA + LANE


def _cparams(sem):
    return pltpu.CompilerParams(dimension_semantics=sem, vmem_limit_bytes=VMEM_LIMIT)


def _dot(a, b):
    return jnp.dot(a, b, preferred_element_type=F32)


def _dot_nt(a, b):
    return lax.dot_general(a, b, (((1,), (1,)), ((), ())), preferred_element_type=F32)


def _sigmoid(x):
    return 1.0 / (1.0 + jnp.exp(-x))


def _silu(x):
    return x * _sigmoid(x)


def _layer_norm(y, g, b):
    mu = jnp.mean(y, axis=-1, keepdims=True)
    yc = y - mu
    var = jnp.mean(yc * yc, axis=-1, keepdims=True)
    return yc * lax.rsqrt(var + EPS) * g + b


def _rms_norm(y, g):
    return y * lax.rsqrt(jnp.mean(y * y, axis=-1, keepdims=True) + EPS) * g


def _split3(a):
    h = a.astype(BF16)
    r = a - h.astype(F32)
    m = r.astype(BF16)
    l = (r - m.astype(F32)).astype(BF16)
    return h, m, l


def _mm3(a, b):
    n = b.shape[1]
    ah = a.astype(BF16)
    al = (a - ah.astype(F32)).astype(BF16)
    bh = b.astype(BF16)
    bl = (b - bh.astype(F32)).astype(BF16)
    lhs = jnp.concatenate([ah, al], axis=1)
    rhs = jnp.concatenate([jnp.concatenate([bh, bl], axis=1),
                           jnp.concatenate([bh, jnp.zeros_like(bl)], axis=1)], axis=0)
    r = _dot(lhs, rhs)
    return r[:, :n] + r[:, n:]


def _ada_kernel(c_ref, w_ref, b_ref, o_ref):
    sc = _silu(c_ref[...]).astype(BF16)
    o_ref[0] = _dot(sc, w_ref[0].astype(BF16)) + b_ref[0]


def _ada_call(c_all, w_ada, b_ada):
    n = c_all.shape[0]
    ncol = w_ada.shape[-1] // D_MODEL
    return pl.pallas_call(
        _ada_kernel,
        grid=(DEPTH, ncol),
        in_specs=[
            pl.BlockSpec((n, D_MODEL), lambda l, j: (0, 0)),
            pl.BlockSpec((1, D_MODEL, D_MODEL), lambda l, j: (l, 0, j)),
            pl.BlockSpec((1, 1, D_MODEL), lambda l, j: (l, 0, j)),
        ],
        out_specs=pl.BlockSpec((1, n, D_MODEL), lambda l, j: (l, 0, j)),
        out_shape=jax.ShapeDtypeStruct((DEPTH, n, ncol * D_MODEL), F32),
        compiler_params=_cparams(("parallel", "parallel")),
        name="ada_mod",
    )(c_all, w_ada, b_ada.reshape(DEPTH, 1, -1))


def _mod_spec(mod3, layer, sub, part, off, bb):
    base = (layer * (mod3.shape[0] // DEPTH) + off) // bb
    col = sub * 3 + part
    return pl.BlockSpec((bb, 1, D_MODEL), lambda i, t: (base + i, 0, col))


def _ffn_kernel(x_ref, sh_ref, sc_ref, gt_ref, win_ref, wout_ref, lng_ref, lnb_ref, o_ref, ub_sc, h_sc,
                *, sub_rows):
    bb, tt, _ = x_ref.shape
    seg = min(tt, sub_rows)
    segs = [(b, t0) for b in range(bb) for t0 in range(0, tt, seg)]
    per_sub = sub_rows // seg

    for k, (b, t0) in enumerate(segs):
        ub_sc[k * seg:(k + 1) * seg] = (x_ref[b, t0:t0 + seg, :] * (1.0 + sc_ref[b]) + sh_ref[b]).astype(BF16)
    for s in range(len(segs) // per_sub):
        rows = slice(s * sub_rows, (s + 1) * sub_rows)
        for c in range(FF_NCH):
            cols = slice(c * FF_CHUNK, (c + 1) * FF_CHUNK)
            gate = _dot(ub_sc[rows], win_ref[:, c * FF_CHUNK:(c + 1) * FF_CHUNK])
            up = _dot(ub_sc[rows], win_ref[:, D_FF + c * FF_CHUNK:D_FF + (c + 1) * FF_CHUNK])
            h_sc[rows, cols] = (_silu(gate) * up).astype(BF16)
        h = _dot(h_sc[rows], wout_ref[...])
        for j in range(per_sub):
            b, t0 = segs[s * per_sub + j]
            y = DEEP_ALPHA * x_ref[b, t0:t0 + seg, :] + 0.5 * (1.0 + gt_ref[b]) * h[j * seg:(j + 1) * seg]
            o_ref[b, t0:t0 + seg, :] = _layer_norm(y, lng_ref[...], lnb_ref[...])


def _ffn_call(x, mod3, layer, sub, off, win_r, wout_r, lng, lnb, bb, tt):
    nb, T, _ = x.shape
    const2 = lambda i, t: (0, 0)
    rows = bb * tt
    sub_rows = min(FF_SUB, rows)
    assert rows % sub_rows == 0 and (tt % sub_rows == 0 or sub_rows % tt == 0)
    tok = pl.BlockSpec((bb, tt, D_MODEL), lambda i, t: (i, t, 0))
    return pl.pallas_call(
        functools.partial(_ffn_kernel, sub_rows=sub_rows),
        grid=(nb // bb, T // tt),
        in_specs=[
            tok,
            _mod_spec(mod3, layer, sub, 0, off, bb),
            _mod_spec(mod3, layer, sub, 1, off, bb),
            _mod_spec(mod3, layer, sub, 2, off, bb),
            pl.BlockSpec(win_r.shape, const2, pipeline_mode=pl.Buffered(1)),
            pl.BlockSpec(wout_r.shape, const2, pipeline_mode=pl.Buffered(1)),
            pl.BlockSpec((1, D_MODEL), const2),
            pl.BlockSpec((1, D_MODEL), const2),
        ],
        out_specs=tok,
        out_shape=jax.ShapeDtypeStruct(x.shape, F32),
        scratch_shapes=[pltpu.VMEM((rows, D_MODEL), BF16), pltpu.VMEM((rows, D_FF), BF16)],
        compiler_params=_cparams(("parallel", "parallel")),
        name="ffn",
    )(x, mod3, mod3, mod3, win_r, wout_r, lng, lnb)


def _e1_kernel(x_ref, sh_ref, sc_ref, w_ref, conv0_ref, convw_ref, cos_ref, sin_ref, cost_ref, sint_ref,
               qn_ref, kvn_ref, wqa_ref, wqb_ref, wkn_ref, wv_ref,
               qkvn_ref, z_ref, bg_ref, qa_ref, ka_ref, vv_ref, ckv_ref, kr_ref, tail_ref, tail_sc,
               *, feature_major):
    bb, tt, _ = x_ref.shape
    rows = bb * tt
    assert bb == 1 or not feature_major

    @pl.when(pl.program_id(1) == 0)
    def _():
        tail_sc[...] = conv0_ref[...]

    u = x_ref[...] * (1.0 + sc_ref[...]) + sh_ref[...]
    ub = u.reshape(rows, D_MODEL).astype(BF16)

    pre = _dot(ub, w_ref[:, E1_QKV:E1_Z])
    for b in range(bb):
        pre_b = pre[b * tt:(b + 1) * tt]
        hist = jnp.concatenate([tail_sc[b], pre_b], axis=0)
        y = pre_b * convw_ref[GDN_CONV - 1:GDN_CONV, :]
        for lag in range(1, GDN_CONV):
            w_row = convw_ref[GDN_CONV - 1 - lag:GDN_CONV - lag, :]
            y = y + pltpu.roll(hist, lag, 0)[SUBLANE:] * w_row
        tail_sc[b] = pre_b[tt - SUBLANE:]
        y = _silu(y)
        for h in range(GDN_H):
            qh = y[:, h * GDN_DK:(h + 1) * GDN_DK]
            qkvn_ref[b, :, h * GDN_DK:(h + 1) * GDN_DK] = (
                qh * lax.rsqrt(jnp.sum(qh * qh, axis=-1, keepdims=True) + EPS) * (GDN_DK ** -0.5))
            ko = GDN_H * GDN_DK + h * GDN_DK
            kh = y[:, ko:ko + GDN_DK]
            qkvn_ref[b, :, ko:ko + GDN_DK] = kh * lax.rsqrt(jnp.sum(kh * kh, axis=-1, keepdims=True) + EPS)
        vo = 2 * GDN_H * GDN_DK
        qkvn_ref[b, :, vo:] = y[:, vo:]
    tail_ref[...] = tail_sc[...]

    z_ref[...] = _dot(ub, w_ref[:, E1_Z:E1_QA]).reshape(bb, tt, -1)
    bg_ref[...] = _dot(ub, w_ref[:, E1_BA:E1_COLS]).reshape(bb, tt, -1)

    cos = cos_ref[...]
    sin = sin_ref[...]
    if bb > 1:
        cos = jnp.concatenate([cos] * bb, axis=0)
        sin = jnp.concatenate([sin] * bb, axis=0)

    cq = _rms_norm(_dot(ub, w_ref[:, E1_QA:E1_CKV]), qn_ref[...])
    if feature_major:
        cq_t = cq.T.astype(BF16)
        qa = _dot(wqa_ref[...], cq_t)
        qb = _dot(wqb_ref[...], cq_t)
        for h in range(MLA_H):
            o = 2 * LANE * h
            qa_ref[0, o:o + LANE, :] = qa[o:o + LANE].astype(BF16)
            rot = qa[o + LANE:o + 2 * LANE] * cost_ref[...] + qb[h * LANE:(h + 1) * LANE] * sint_ref[...]
            qa_ref[0, o + LANE:o + 2 * LANE, :] = rot.astype(BF16)
    else:
        cqb = cq.astype(BF16)
        qa = _dot(cqb, wqa_ref[...])
        qb = _dot(cqb, wqb_ref[...])
        for h in range(MLA_H):
            o = 2 * LANE * h
            qa_ref[:, :, o:o + LANE] = qa[:, o:o + LANE].astype(BF16).reshape(bb, tt, LANE)
            rot = qa[:, o + LANE:o + 2 * LANE] * cos + qb[:, h * LANE:(h + 1) * LANE] * sin
            qa_ref[:, :, o + LANE:o + 2 * LANE] = rot.astype(BF16).reshape(bb, tt, LANE)

    ckv = _rms_norm(_dot(ub, w_ref[:, E1_CKV:E1_KRA]), kvn_ref[...])
    ckv_ref[...] = ckv.reshape(bb, tt, -1)
    kr = _dot(ub, w_ref[:, E1_KRA:E1_KRB]) * cos + _dot(ub, w_ref[:, E1_KRB:E1_BA]) * sin
    kr_ref[...] = kr[:, :MLA_ROPE].reshape(bb, tt, -1)
    cb = ckv.astype(BF16)
    krb = kr.astype(BF16).reshape(bb, tt, LANE)
    kn = _dot(cb, wkn_ref[...])
    for h in range(MLA_H):
        o = 2 * LANE * h
        ka_ref[:, :, o:o + LANE] = kn[:, h * LANE:(h + 1) * LANE].astype(BF16).reshape(bb, tt, LANE)
        ka_ref[:, :, o + LANE:o + 2 * LANE] = krb
    if feature_major:
        vv_ref[0] = _dot(wv_ref[...], ckv.T.astype(BF16)).astype(BF16)
    else:
        vv_ref[...] = _dot(cb, wv_ref[...]).astype(BF16).reshape(bb, tt, -1)


def _e1_call(x, mod3, layer, off, ew, conv0p, rope, bb, tt, feature_major):
    nb, T, _ = x.shape
    tok = lambda i, t: (i, t, 0)
    const2 = lambda i, t: (0, 0)

    def full(a):
        return pl.BlockSpec(a.shape, const2)

    out_dims = [(GDN_QKV, F32), (GDN_H * GDN_DV, F32), (LANE, F32), (MLA_H * 2 * LANE, BF16),
                (MLA_H * 2 * LANE, BF16), (MLA_H * MLA_V, BF16), (MLA_KV_LORA, F32), (MLA_ROPE, F32)]
    out_shape = [jax.ShapeDtypeStruct((nb, T, d), dt) for d, dt in out_dims]
    out_specs = [pl.BlockSpec((bb, tt, d), tok) for d, _ in out_dims]
    wqa, wqb, wv = ew["wqa"], ew["wqb"], ew["wv"]
    if feature_major:
        wqa, wqb, wv = ew["wqa_t"], ew["wqb_t"], ew["wv_t"]
        for idx in (3, 5):
            d = out_dims[idx][0]
            out_shape[idx] = jax.ShapeDtypeStruct((nb, d, T), BF16)
            out_specs[idx] = pl.BlockSpec((bb, d, tt), lambda i, t: (i, 0, t))
    out_shape.append(jax.ShapeDtypeStruct((nb, SUBLANE, GDN_QKV), F32))
    out_specs.append(pl.BlockSpec((bb, SUBLANE, GDN_QKV), lambda i, t: (i, 0, 0)))
    cos_t, sin_t = rope["rows"]
    cos_f, sin_f = rope["cols"]
    return pl.pallas_call(
        functools.partial(_e1_kernel, feature_major=feature_major),
        grid=(nb // bb, T // tt),
        in_specs=[
            pl.BlockSpec((bb, tt, D_MODEL), tok),
            _mod_spec(mod3, layer, 1, 0, off, bb),
            _mod_spec(mod3, layer, 1, 1, off, bb),
            full(ew["w_e1"]),
            pl.BlockSpec((bb, SUBLANE, GDN_QKV), lambda i, t: (i, 0, 0)),
            full(ew["conv_w"]),
            pl.BlockSpec((tt, LANE), lambda i, t: (t, 0)),
            pl.BlockSpec((tt, LANE), lambda i, t: (t, 0)),
            pl.BlockSpec((LANE, tt), lambda i, t: (0, t)),
            pl.BlockSpec((LANE, tt), lambda i, t: (0, t)),
            full(ew["q_norm"]), full(ew["kv_norm"]),
            full(wqa), full(wqb), full(ew["wkn"]), full(wv),
        ],
        out_specs=out_specs,
        out_shape=out_shape,
        scratch_shapes=[pltpu.VMEM((bb, SUBLANE, GDN_QKV), F32)],
        compiler_params=_cparams(("parallel", "arbitrary")),
        name="mix_proj",
    )(x, mod3, mod3, ew["w_e1"], conv0p, ew["conv_w"], cos_t, sin_t, cos_f, sin_f, ew["q_norm"], ew["kv_norm"],
      wqa, wqb, ew["wkn"], wv)


def _gdn_kernel(qkvn_ref, z_ref, bg_ref, s0_ref, alog_ref, dtb_ref, nw_ref, o_ref, s_ref, s_sc, *, n_valid):
    bb = qkvn_ref.shape[0]
    C = GDN_C
    units = [(b, h) for b in range(bb) for h in range(GDN_H)]

    @pl.when(pl.program_id(1) == 0)
    def _():
        s_sc[...] = s0_ref[...]

    row = lax.broadcasted_iota(jnp.int32, (C, C), 0)
    col = lax.broadcasted_iota(jnp.int32, (C, C), 1)
    incl = row >= col
    strict = row > col
    tri = jnp.where(incl, 1.0, 0.0).astype(BF16)
    eye = jnp.where(row == col, 1.0, 0.0)

    beta_all, G_all, G_t = [], [], []
    for b in range(bb):
        bg = bg_ref[b]
        beta_all.append(_sigmoid(bg))
        xs = bg + dtb_ref[...]
        softplus = jnp.maximum(xs, 0.0) + jnp.log(1.0 + jnp.exp(-jnp.abs(xs)))
        g_all = -jnp.exp(alog_ref[...]) * softplus
        if n_valid < C:
            g_all = jnp.where(row < n_valid, g_all, 0.0)
        g3 = _dot(tri, jnp.concatenate(_split3(g_all), axis=1))
        G_all.append(g3[:, :LANE] + g3[:, LANE:2 * LANE] + g3[:, 2 * LANE:])
        G_t.append(G_all[b].T)

    def load(b, h, part):
        o = (part * GDN_H + h) * GDN_DK
        return qkvn_ref[b, :, o:o + GDN_DK]

    q = [load(b, h, 0) for b, h in units]
    k = [load(b, h, 1) for b, h in units]
    v = [load(b, h, 2) for b, h in units]
    kb = [x.astype(BF16) for x in k]
    beta = [beta_all[b][:, h:h + 1] for b, h in units]
    Gc = [G_all[b][:, GDN_H + h:GDN_H + h + 1] for b, h in units]
    Gr = [G_t[b][GDN_H + h:GDN_H + h + 1, :] for b, h in units]
    decay = [jnp.where(incl, jnp.exp(jnp.where(incl, c - r, 0.0)), 0.0) for c, r in zip(Gc, Gr)]
    kk = [_dot_nt(x, x) for x in kb]
    lower = [jnp.where(strict, bt * d * m, 0.0) for bt, d, m in zip(beta, decay, kk)]
    inv = [eye - m for m in lower]
    pw = [_mm3(m, m) for m in lower]
    n_sq = int(math.log2(C)) - 1
    for it in range(n_sq):
        inv = [a + _mm3(a, p) for a, p in zip(inv, pw)]
        if it + 1 < n_sq:
            pw = [_mm3(p, p) for p in pw]
    gam = [jnp.exp(c) for c in Gc]
    sol_v = [_mm3(a, bt * x) for a, bt, x in zip(inv, beta, v)]
    sol_k = [_mm3(a, (bt * g) * x).astype(BF16) for a, bt, g, x in zip(inv, beta, gam, k)]
    attn = [(_dot_nt(x.astype(BF16), y) * d).astype(BF16) for x, y, d in zip(q, kb, decay)]
    qg = [(g * x).astype(BF16) for g, x in zip(gam, q)]
    g_end = [c[C - 1:C, :] for c in Gc]
    kd = [(jnp.exp(e - c) * x).T.astype(BF16) for e, c, x in zip(g_end, Gc, k)]

    s0 = [s_sc[b, h] for b, h in units]
    s0b = [x.astype(BF16) for x in s0]
    ub = [(a - _dot(w, s)).astype(BF16) for a, w, s in zip(sol_v, sol_k, s0b)]
    o = [_dot(x, s) + _dot(a, y) for x, s, a, y in zip(qg, s0b, attn, ub)]
    s1 = [jnp.exp(e) * s + _dot(x, y) for e, s, x, y in zip(g_end, s0, kd, ub)]
    for i, (b, h) in enumerate(units):
        s_sc[b, h] = s1[i]
        zh = z_ref[b, :, h * GDN_DV:(h + 1) * GDN_DV]
        o_ref[b, :, h * GDN_DV:(h + 1) * GDN_DV] = (_rms_norm(o[i], nw_ref[...]) * _silu(zh)).astype(BF16)
    s_ref[...] = s_sc[...]


def _gdn_call(qkvn, z, bg, s0, ew, bb, n_valid):
    nb, T, _ = qkvn.shape
    tok = lambda i, c: (i, c, 0)
    const2 = lambda i, c: (0, 0)
    st = lambda i, c: (i, 0, 0, 0)
    return pl.pallas_call(
        functools.partial(_gdn_kernel, n_valid=n_valid),
        grid=(nb // bb, T // GDN_C),
        in_specs=[
            pl.BlockSpec((bb, GDN_C, GDN_QKV), tok),
            pl.BlockSpec((bb, GDN_C, GDN_H * GDN_DV), tok),
            pl.BlockSpec((bb, GDN_C, LANE), tok),
            pl.BlockSpec((bb, GDN_H, GDN_DK, GDN_DV), st),
            pl.BlockSpec((1, LANE), const2),
            pl.BlockSpec((1, LANE), const2),
            pl.BlockSpec((1, GDN_DV), const2),
        ],
        out_specs=[pl.BlockSpec((bb, GDN_C, GDN_H * GDN_DV), tok),
                   pl.BlockSpec((bb, GDN_H, GDN_DK, GDN_DV), st)],
        out_shape=[jax.ShapeDtypeStruct((nb, T, GDN_H * GDN_DV), BF16),
                   jax.ShapeDtypeStruct((nb, GDN_H, GDN_DK, GDN_DV), F32)],
        scratch_shapes=[pltpu.VMEM((bb, GDN_H, GDN_DK, GDN_DV), F32)],
        compiler_params=_cparams(("parallel", "arbitrary")),
        name="gdn",
    )(qkvn, z, bg, s0, ew["alog"], ew["dtb"], ew["norm_w"])


def _attn_init(m_sc, l_sc, acc_sc):
    m_sc[...] = jnp.full_like(m_sc, -jnp.inf)
    l_sc[...] = jnp.zeros_like(l_sc)
    acc_sc[...] = jnp.zeros_like(acc_sc)


def _mix_out(og, heads_out, x_ref, gt_ref, wout_ref, lng_ref, lnb_ref, o_ref):
    mix = _dot(jnp.concatenate([og] + heads_out, axis=-1), wout_ref[...])
    y = DEEP_ALPHA * x_ref[0] + (1.0 + gt_ref[0]) * mix
    o_ref[0] = _layer_norm(y, lng_ref[...], lnb_ref[...])


def _mla_prompt_kernel(qa_ref, ka_ref, vv_ref, og_ref, x_ref, gt_ref, wout_ref, lng_ref, lnb_ref, o_ref,
                       m_sc, l_sc, acc_sc):
    tq = qa_ref.shape[2]
    i = pl.program_id(1)
    _attn_init(m_sc, l_sc, acc_sc)
    heads = range(MLA_H)
    vis = (lax.broadcasted_iota(jnp.int32, (tq, tq), 1) // CHUNK
           >= lax.broadcasted_iota(jnp.int32, (tq, tq), 0) // CHUNK)

    def block(j, masked):
        ks = pl.multiple_of(j * tq, tq)
        s = [_dot(ka_ref[0, pl.ds(ks, tq), 2 * LANE * h:2 * LANE * (h + 1)],
                  qa_ref[0, 2 * LANE * h:2 * LANE * (h + 1), :]) * (MLA_SCALE * LOG2_E)
             for h in heads]
        if masked:
            s = [jnp.where(vis, x, -jnp.inf) for x in s]
        m_prev = [m_sc[h] for h in heads]
        m_new = [jnp.maximum(m, jnp.max(x, axis=0, keepdims=True)) for m, x in zip(m_prev, s)]
        p = [jnp.exp2(x - m) for x, m in zip(s, m_new)]
        alpha = [jnp.exp2(a - b) for a, b in zip(m_prev, m_new)]
        pv = [_dot(vv_ref[0, MLA_V * h:MLA_V * (h + 1), pl.ds(ks, tq)], p[h].astype(BF16)) for h in heads]
        for h in heads:
            l_sc[h] = alpha[h] * l_sc[h] + jnp.sum(p[h], axis=0, keepdims=True)
            acc_sc[h] = alpha[h] * acc_sc[h] + pv[h]
            m_sc[h] = m_new[h]

    def body(j, carry):
        block(j, False)
        return carry

    lax.fori_loop(0, i, body, 0)
    block(i, True)
    heads_out = [(acc_sc[h] / l_sc[h]).T.astype(BF16) for h in heads]
    _mix_out(og_ref[0], heads_out, x_ref, gt_ref, wout_ref, lng_ref, lnb_ref, o_ref)


def _mla_sample_kernel(qa_ref, pckv_ref, pkr_ref, ka_ref, vv_ref, og_ref, x_ref, gt_ref, wknt_ref, wv_ref,
                       wout_ref, lng_ref, lnb_ref, o_ref):
    ckv = pckv_ref[0]
    kn_t = _dot(wknt_ref[...], ckv.T.astype(BF16)).astype(BF16)
    kr_t = pkr_ref[0].T.astype(BF16)
    vp = _dot(ckv.astype(BF16), wv_ref[...]).astype(BF16)
    heads_out = []
    for h in range(MLA_H):
        q = qa_ref[0, :, 2 * LANE * h:2 * LANE * (h + 1)]
        k_t = jnp.concatenate([kn_t[h * MLA_NOPE:(h + 1) * MLA_NOPE], kr_t], axis=0)
        s_p = _dot(q, k_t) * (MLA_SCALE * LOG2_E)
        s_n = _dot_nt(q, ka_ref[0, :, 2 * LANE * h:2 * LANE * (h + 1)]) * (MLA_SCALE * LOG2_E)
        m = jnp.maximum(jnp.max(s_p, axis=-1, keepdims=True), jnp.max(s_n, axis=-1, keepdims=True))
        p_p = jnp.exp2(s_p - m)
        p_n = jnp.exp2(s_n - m)
        l = jnp.sum(p_p, axis=-1, keepdims=True) + jnp.sum(p_n, axis=-1, keepdims=True)
        o = (_dot(p_p.astype(BF16), vp[:, MLA_V * h:MLA_V * (h + 1)])
             + _dot(p_n.astype(BF16), vv_ref[0, :, MLA_V * h:MLA_V * (h + 1)]))
        heads_out.append((o / l).astype(BF16))
    _mix_out(og_ref[0], heads_out, x_ref, gt_ref, wout_ref, lng_ref, lnb_ref, o_ref)


def _mla_prompt_call(qa, ka, vv, og, x, mod3, layer, off, w_out, lng, lnb, tq):
    nb, T, _ = x.shape
    tok = lambda b, i: (b, i, 0)
    seq = lambda b, i: (b, 0, 0)
    const2 = lambda b, i: (0, 0)
    return pl.pallas_call(
        _mla_prompt_kernel,
        grid=(nb, T // tq),
        in_specs=[
            pl.BlockSpec((1, qa.shape[1], tq), lambda b, i: (b, 0, i)),
            pl.BlockSpec((1,) + ka.shape[1:], seq),
            pl.BlockSpec((1,) + vv.shape[1:], seq),
            pl.BlockSpec((1, tq, og.shape[-1]), tok),
            pl.BlockSpec((1, tq, D_MODEL), tok),
            _mod_spec(mod3, layer, 1, 2, off, 1),
            pl.BlockSpec(w_out.shape, const2),
            pl.BlockSpec((1, D_MODEL), const2),
            pl.BlockSpec((1, D_MODEL), const2),
        ],
        out_specs=pl.BlockSpec((1, tq, D_MODEL), tok),
        out_shape=jax.ShapeDtypeStruct(x.shape, F32),
        scratch_shapes=[pltpu.VMEM((MLA_H, 1, tq), F32), pltpu.VMEM((MLA_H, 1, tq), F32),
                        pltpu.VMEM((MLA_H, MLA_V, tq), F32)],
        compiler_params=_cparams(("parallel", "arbitrary")),
        name="mla_prompt",
    )(qa, ka, vv, og, x, mod3, w_out, lng, lnb)


def _mla_sample_call(qa, past_ckv, past_kr128, ka, vv, og, x, mod3, layer, off, ew, lng, lnb):
    nb, T, _ = x.shape
    seq = lambda b, i: (b, 0, 0)
    const2 = lambda b, i: (0, 0)

    def whole(a):
        return pl.BlockSpec((1,) + a.shape[1:], seq)

    def full(a):
        return pl.BlockSpec(a.shape, const2)

    return pl.pallas_call(
        _mla_sample_kernel,
        grid=(nb, 1),
        in_specs=[whole(qa), whole(past_ckv), whole(past_kr128), whole(ka), whole(vv), whole(og), whole(x),
                  _mod_spec(mod3, layer, 1, 2, off, 1),
                  full(ew["wkn_t"]), full(ew["wv"]), full(ew["w_out"]), full(lng), full(lnb)],
        out_specs=whole(x),
        out_shape=jax.ShapeDtypeStruct(x.shape, F32),
        compiler_params=_cparams(("parallel", "arbitrary")),
        name="mla_sample",
    )(qa, past_ckv, past_kr128, ka, vv, og, x, mod3, ew["wkn_t"], ew["wv"], ew["w_out"], lng, lnb)


def _s5prep_kernel(lre_ref, lim_ref, ldt_ref, br_ref, bi_ref, abre_ref, abim_ref, bbre_ref, bbim_ref):
    dt = jnp.exp(ldt_ref[0])
    lr = jnp.minimum(lre_ref[0], -1e-4)
    li = lim_ref[0]
    mag = jnp.exp(lr * dt)
    ab_re = mag * jnp.cos(li * dt)
    ab_im = mag * jnp.sin(li * dt)
    den = lr * lr + li * li
    coef_re = ((ab_re - 1.0) * lr + ab_im * li) / den
    coef_im = (ab_im * lr - (ab_re - 1.0) * li) / den
    br = br_ref[0]
    bi = bi_ref[0]
    abre_ref[0] = ab_re
    abim_ref[0] = ab_im
    bbre_ref[0] = coef_re * br - coef_im * bi
    bbim_ref[0] = coef_re * bi + coef_im * br


def _s5prep_call(lre_x, lim_x, ldt_x, br_t, bi_t):
    n = lre_x.shape[0]
    blk = pl.BlockSpec((1, D_MODEL, S5_P), lambda i: (i, 0, 0))
    return pl.pallas_call(
        _s5prep_kernel,
        grid=(n,),
        in_specs=[blk, blk, pl.BlockSpec((1, D_MODEL, 1), lambda i: (i, 0, 0)), blk, blk],
        out_specs=[blk] * 4,
        out_shape=[jax.ShapeDtypeStruct((n, D_MODEL, S5_P), F32)] * 4,
        compiler_params=_cparams(("parallel",)),
        name="s5_discretise",
    )(lre_x, lim_x, ldt_x, br_t, bi_t)


def _s5_kernel(x_ref, sh_ref, sc_ref, gt_ref, h0re_ref, h0im_ref, abre_ref, abim_ref, bb_ref, cre_ref, cim_ref,
               dskip_ref, wglu_ref, wout_ref, lng_ref, lnb_ref, o_ref, hre_ref, him_ref,
               hre_sc, him_sc, xt_sc, ub_sc, st_sc, y_sc):
    nb, tt, _ = x_ref.shape
    rows = tt * nb
    half = S5_SPT // 2
    xt_sc[...] = pltpu.einshape("btd->tbd", x_ref[...])

    @pl.when(pl.program_id(0) == 0)
    def _():
        hre_sc[...] = h0re_ref[...]
        him_sc[...] = h0im_ref[...]

    def modulated():
        return (xt_sc[...] * (1.0 + sc_ref[...]) + sh_ref[...]).reshape(rows, D_MODEL)

    ub_sc[...] = modulated().astype(BF16)
    for kt in range(S5_NKT):
        st_sc[kt] = _dot(ub_sc[:, kt * S5_KT:(kt + 1) * S5_KT], bb_ref[kt])
    for kt in range(S5_NKT):
        for lc in range(2):
            so = kt * S5_SPT + lc * half
            re_cols = slice(lc * half, (lc + 1) * half)
            im_cols = slice(S5_SPT + lc * half, S5_SPT + (lc + 1) * half)
            a_re = jnp.broadcast_to(abre_ref[:, so:so + half], (nb, half))
            a_im = jnp.broadcast_to(abim_ref[:, so:so + half], (nb, half))
            h_re = hre_sc[:, so:so + half]
            h_im = him_sc[:, so:so + half]
            for t in range(tt):
                frame = slice(t * nb, (t + 1) * nb)
                n_re = a_re * h_re - a_im * h_im + st_sc[kt, frame, re_cols]
                n_im = a_re * h_im + a_im * h_re + st_sc[kt, frame, im_cols]
                st_sc[kt, frame, re_cols] = n_re
                st_sc[kt, frame, im_cols] = n_im
                h_re, h_im = n_re, n_im
            hre_sc[:, so:so + half] = h_re
            him_sc[:, so:so + half] = h_im
        y_sc[:, kt * S5_KT:(kt + 1) * S5_KT] = (
            _dot(st_sc[kt, :, :S5_SPT].astype(BF16), cre_ref[kt])
            - _dot(st_sc[kt, :, S5_SPT:].astype(BF16), cim_ref[kt]))

    y = y_sc[...] + dskip_ref[...] * modulated()
    yg = 0.5 * y * (1.0 + jnp.tanh(math.sqrt(2.0 / math.pi) * (y + 0.044715 * (y * y * y))))
    glu = _dot(yg.astype(BF16), wglu_ref[...])
    mix = _dot((yg * _sigmoid(glu)).astype(BF16), wout_ref[...])
    out = DEEP_ALPHA * xt_sc[...] + (1.0 + gt_ref[...]) * mix.reshape(tt, nb, D_MODEL)
    o_ref[...] = pltpu.einshape("tbd->btd", _layer_norm(out, lng_ref[...], lnb_ref[...]))

    @pl.when(pl.program_id(0) == pl.num_programs(0) - 1)
    def _():
        hre_ref[...] = hre_sc[...]
        him_ref[...] = him_sc[...]


def _s5_call(x, mod, layer, stream, h0re, h0im, sw, lng, lnb, tt):
    nb, T, _ = x.shape
    rows = tt * nb
    const2 = lambda t: (0, 0)
    const3 = lambda t: (0, 0, 0)

    def mspec(part):
        return pl.BlockSpec((1, nb, D_MODEL), lambda t: (layer, stream, 3 + part))

    def full(a):
        return pl.BlockSpec(a.shape, const2 if a.ndim == 2 else const3, pipeline_mode=pl.Buffered(1))

    return pl.pallas_call(
        _s5_kernel,
        grid=(T // tt,),
        in_specs=[
            pl.BlockSpec((nb, tt, D_MODEL), lambda t: (0, t, 0)),
            mspec(0), mspec(1), mspec(2),
            full(h0re), full(h0im), full(sw["ab_re"]), full(sw["ab_im"]),
            full(sw["bb"]), full(sw["c_re"]), full(sw["c_im"]), full(sw["d_skip"]),
            full(sw["w_glu"]), full(sw["w_out"]), full(lng), full(lnb),
        ],
        out_specs=[pl.BlockSpec((nb, tt, D_MODEL), lambda t: (0, t, 0)),
                   pl.BlockSpec((nb, S5_STATE), const2), pl.BlockSpec((nb, S5_STATE), const2)],
        out_shape=[jax.ShapeDtypeStruct(x.shape, F32),
                   jax.ShapeDtypeStruct((nb, S5_STATE), F32), jax.ShapeDtypeStruct((nb, S5_STATE), F32)],
        scratch_shapes=[pltpu.VMEM((nb, S5_STATE), F32), pltpu.VMEM((nb, S5_STATE), F32),
                        pltpu.VMEM((tt, nb, D_MODEL), F32), pltpu.VMEM((rows, D_MODEL), BF16), pltpu.VMEM((S5_NKT, rows, 2 * S5_SPT), F32),
                        pltpu.VMEM((rows, D_MODEL), F32)],
        compiler_params=_cparams(("arbitrary",)),
        name="s5_mixer",
    )(x, mod, mod, mod, h0re, h0im, sw["ab_re"], sw["ab_im"], sw["bb"], sw["c_re"], sw["c_im"],
      sw["d_skip"], sw["w_glu"], sw["w_out"], lng, lnb)


def _prep_ffn(w_in, w_out):
    return w_in.astype(BF16), w_out.astype(BF16)


def _prep_even(w_in, conv_w, a_log, dt_bias, norm_w, q_norm, w_q_b, kv_norm, w_kv_b, w_out):
    half = MLA_ROPE // 2
    zcol = lambda n: jnp.zeros((w_in.shape[0], n), w_in.dtype)
    k1 = w_in[:, OFF_KVA + MLA_KV_LORA:OFF_KVA + MLA_KV_LORA + half]
    k2 = w_in[:, OFF_KVA + MLA_KV_LORA + half:OFF_KVA + MLA_KV_LORA + MLA_ROPE]
    w_e1 = jnp.concatenate([
        w_in[:, :OFF_B],
        w_in[:, OFF_QA:OFF_KVA + MLA_KV_LORA],
        k1, k2, zcol(LANE - MLA_ROPE),
        -k2, k1, zcol(LANE - MLA_ROPE),
        w_in[:, OFF_B:OFF_QA], zcol(LANE - 2 * GDN_H),
    ], axis=1).astype(BF16)
    zq = jnp.zeros((MLA_Q_LORA, LANE - MLA_ROPE), w_q_b.dtype)
    qa_cols, qb_cols = [], []
    for h in range(MLA_H):
        o = h * (MLA_NOPE + MLA_ROPE)
        r1 = w_q_b[:, o + MLA_NOPE:o + MLA_NOPE + half]
        r2 = w_q_b[:, o + MLA_NOPE + half:o + MLA_NOPE + MLA_ROPE]
        qa_cols += [w_q_b[:, o:o + MLA_NOPE], r1, r2, zq]
        qb_cols += [-r2, r1, zq]
    kv = w_kv_b.reshape(MLA_KV_LORA, MLA_H, MLA_NOPE + MLA_V)
    pad = lambda v: jnp.pad(v.astype(F32), (GDN_H, LANE - 2 * GDN_H)).reshape(1, LANE)
    return {
        "w_e1": w_e1,
        "conv_w": conv_w.astype(F32),
        "alog": pad(a_log), "dtb": pad(dt_bias),
        "norm_w": norm_w.reshape(1, -1).astype(F32),
        "q_norm": q_norm.reshape(1, -1).astype(F32),
        "kv_norm": kv_norm.reshape(1, -1).astype(F32),
        "wqa": jnp.concatenate(qa_cols, axis=1).astype(BF16),
        "wqb": jnp.concatenate(qb_cols, axis=1).astype(BF16),
        "wqa_t": jnp.concatenate(qa_cols, axis=1).T.astype(BF16),
        "wqb_t": jnp.concatenate(qb_cols, axis=1).T.astype(BF16),
        "wv_t": kv[:, :, MLA_NOPE:].reshape(MLA_KV_LORA, -1).T.astype(BF16),
        "wkn": kv[:, :, :MLA_NOPE].reshape(MLA_KV_LORA, -1).astype(BF16),
        "wkn_t": kv[:, :, :MLA_NOPE].reshape(MLA_KV_LORA, -1).T.astype(BF16),
        "wv": kv[:, :, MLA_NOPE:].reshape(MLA_KV_LORA, -1).astype(BF16),
        "w_out": w_out.astype(BF16),
    }


def _rope_tables(pos):
    half = MLA_ROPE // 2
    inv = ROPE_THETA ** (-jnp.arange(half, dtype=F32) / half)
    ang = pos.astype(F32)[:, None] * inv[None, :]
    z = jnp.zeros((pos.shape[0], LANE - MLA_ROPE), F32)
    cos, sin = jnp.cos(ang), jnp.sin(ang)
    cos, sin = jnp.concatenate([cos, cos, z], axis=1), jnp.concatenate([sin, sin, z], axis=1)
    return {"rows": (cos, sin), "cols": (cos.T, sin.T)}


def _prep_s5(lam_re, lam_im, log_dt, b_re, b_im, c_re, c_im, d_skip, w_glu, w_out):
    n = lam_re.shape[0]
    rep = lambda a: jnp.repeat(a.astype(F32), S5_M, axis=1)
    to_gmp = lambda a: jnp.transpose(a.astype(F32), (0, 1, 3, 2)).reshape(n, D_MODEL, S5_P)
    ab_re, ab_im, bb_re, bb_im = _s5prep_call(
        rep(lam_re), rep(lam_im), jnp.repeat(log_dt.astype(F32), S5_M, axis=1)[..., None],
        to_gmp(b_re), to_gmp(b_im))
    eye = jnp.eye(S5_GPT, dtype=F32)

    def embed_in(bb):
        t = bb.reshape(n, S5_NKT, S5_GPT, S5_M, S5_P)
        return jnp.einsum("nkgmp,gh->nkgmhp", t, eye).reshape(n, S5_NKT, S5_KT, S5_SPT)

    def embed_out(c):
        t = c.astype(F32).reshape(n, S5_NKT, S5_GPT, S5_M, S5_P)
        return jnp.einsum("nkgmp,gh->nkhpgm", t, eye).reshape(n, S5_NKT, S5_SPT, S5_KT)

    pick = lambda a: a.reshape(n, S5_G, S5_M, S5_P)[:, :, 0, :].reshape(n, 1, S5_STATE)
    return {
        "ab_re": pick(ab_re), "ab_im": pick(ab_im),
        "bb": jnp.concatenate([embed_in(bb_re), embed_in(bb_im)], axis=-1).astype(BF16),
        "c_re": embed_out(c_re).astype(BF16), "c_im": embed_out(c_im).astype(BF16),
        "d_skip": d_skip.reshape(n, 1, D_MODEL).astype(F32),
        "w_glu": w_glu.astype(BF16), "w_out": w_out.astype(BF16),
    }


def _pad_rows(a, n):
    return jnp.pad(a, ((0, 0), (0, n - a.shape[1]), (0, 0)))


def _run_trunk(x, off, pos, mod, mod3, ffw, evw, s5w, ln_g, ln_b, gdn0, conv0, s5re0, s5im0, past, cfg):
    nb, T, _ = x.shape
    rope = _rope_tables(pos)
    new_ckv, new_kr, new_gdn, new_conv, new_re, new_im = [], [], [], [], [], []
    stream = off // nb
    for l in range(DEPTH):
        lng = lambda s: ln_g[l, s].reshape(1, D_MODEL)
        lnb = lambda s: ln_b[l, s].reshape(1, D_MODEL)
        x = _ffn_call(x, mod3, l, 0, off, *ffw[l][0], lng(0), lnb(0), cfg["ffn_bb"], cfg["ffn_tt"])
        i = l // 2
        if l % 2 == 0:
            ew = evw[i]
            conv0p = jnp.pad(conv0[i].astype(F32), ((0, 0), (SUBLANE - (GDN_CONV - 1), 0), (0, 0)))
            qkvn, z, bg, qa, ka, vv, ckv, kr, tail = _e1_call(
                x, mod3, l, off, ew, conv0p, rope, cfg["e1_bb"], cfg["e1_tt"], past is None)
            n_valid = min(T, GDN_C)
            if T < GDN_C:
                qkvn, z, bg = (_pad_rows(a, GDN_C) for a in (qkvn, z, bg))
            og, s_new = _gdn_call(qkvn, z, bg, gdn0[i].astype(F32), ew, cfg["gdn_bb"], n_valid)
            og = og[:, :T]
            if past is None:
                x = _mla_prompt_call(qa, ka, vv, og, x, mod3, l, off, ew["w_out"], lng(1), lnb(1), cfg["mla_tq"])
            else:
                x = _mla_sample_call(qa, past[0][i], past[1][i], ka, vv, og, x, mod3, l, off, ew, lng(1), lnb(1))
            new_gdn.append(s_new)
            new_conv.append(tail[:, SUBLANE - (GDN_CONV - 1):])
            new_ckv.append(ckv)
            new_kr.append(kr)
        else:
            sw = {k: v[i] for k, v in s5w.items()}
            x, h_re, h_im = _s5_call(x, mod, l, stream, s5re0[i].reshape(nb, S5_STATE).astype(F32),
                                     s5im0[i].reshape(nb, S5_STATE).astype(F32), sw, lng(1), lnb(1), cfg["s5_tt"])
            new_re.append(h_re.reshape(nb, S5_G, S5_P))
            new_im.append(h_im.reshape(nb, S5_G, S5_P))
        x = _ffn_call(x, mod3, l, 2, off, *ffw[l][1], lng(2), lnb(2), cfg["ffn_bb"], cfg["ffn_tt"])
    return (x, jnp.stack(new_ckv), jnp.stack(new_kr), jnp.stack(new_gdn), jnp.stack(new_conv),
            jnp.stack(new_re), jnp.stack(new_im))


def _config(nb, T):
    if T >= 512:
        return {"ffn_bb": 1, "ffn_tt": 2 * FF_SUB, "e1_bb": 1, "e1_tt": 512, "gdn_bb": 2, "mla_tq": 512,
                "s5_tt": 32}
    bb = max(1, min(nb, 512 // T))
    return {"ffn_bb": max(1, min(nb, 2 * FF_SUB // T)), "ffn_tt": T, "e1_bb": bb, "e1_tt": T, "gdn_bb": 2,
            "mla_tq": T, "s5_tt": min(T, 32)}


def kernel(x_prompt, x_sample, c_prompt, c_sample, cache_mla_latent, cache_mla_krope, state_gdn, state_gdn_conv, state_s5_re, state_s5_im, w_ada, b_ada, ln_g, ln_b, w_ff_in, w_ff_out, w_mix_in, gdn_conv_w, gdn_a_log, gdn_dt_bias, gdn_norm_w, mla_q_norm, mla_w_q_b, mla_kv_norm, mla_w_kv_b, w_mix_out, s5_lambda_re, s5_lambda_im, s5_log_dt, s5_b_re, s5_b_im, s5_c_re, s5_c_im, s5_d, s5_w_glu, s5_w_out):
    bp, sp, _ = x_prompt.shape
    bs, ss, _ = x_sample.shape
    assert bp == bs, "the modulation table is indexed in blocks of one stream's batch"

    mod = _ada_call(jnp.concatenate([c_prompt, c_sample], axis=0), w_ada, b_ada)
    mod3 = mod.reshape(DEPTH * (bp + bs), 1, -1)

    ffw = [[_prep_ffn(w_ff_in[l, s], w_ff_out[l, s]) for s in range(2)] for l in range(DEPTH)]
    evw = [_prep_even(w_mix_in[i], gdn_conv_w[i], gdn_a_log[i], gdn_dt_bias[i], gdn_norm_w[i], mla_q_norm[i],
                      mla_w_q_b[i], mla_kv_norm[i], mla_w_kv_b[i], w_mix_out[i]) for i in range(N_EVEN)]
    s5w = _prep_s5(s5_lambda_re, s5_lambda_im, s5_log_dt, s5_b_re, s5_b_im, s5_c_re, s5_c_im, s5_d,
                   s5_w_glu, s5_w_out)

    zeros = lambda *s: jnp.zeros(s, F32)
    outs_p = _run_trunk(
        x_prompt, 0, jnp.arange(sp, dtype=jnp.int32), mod, mod3, ffw, evw, s5w, ln_g, ln_b,
        zeros(N_EVEN, bp, GDN_H, GDN_DK, GDN_DV), zeros(N_EVEN, bp, GDN_CONV - 1, GDN_QKV),
        zeros(N_ODD, bp, S5_G, S5_P), zeros(N_ODD, bp, S5_G, S5_P), None, _config(bp, sp))

    n_past = cache_mla_latent.shape[2]
    past_kr = jnp.pad(cache_mla_krope, ((0, 0), (0, 0), (0, 0), (0, LANE - MLA_ROPE)))
    outs_s = _run_trunk(
        x_sample, bp, n_past + jnp.arange(ss, dtype=jnp.int32), mod, mod3, ffw, evw, s5w, ln_g, ln_b,
        state_gdn, state_gdn_conv, state_s5_re, state_s5_im, (cache_mla_latent, past_kr), _config(bs, ss))

    y_p, lat_p, kr_p, gdn_p, conv_p, re_p, im_p = outs_p
    y_s, lat_s, kr_s, gdn_s, conv_s, re_s, im_s = outs_s
    return (y_p, y_s, lat_p, kr_p, gdn_p, conv_p, re_p, im_p, lat_s, kr_s, gdn_s, conv_s, re_s, im_s)
```

```python
import functools
import math

import jax
import jax.numpy as jnp
from jax import lax
from jax.experimental import pallas as pl
from jax.experimental.pallas import tpu as pltpu

F32 = jnp.float32
BF16 = jnp.bfloat16

D_MODEL = 1024
DEPTH = 4
CHUNK = 64
N_EVEN = (DEPTH + 1) // 2
N_ODD = DEPTH // 2
DEEP_ALPHA = (2.0 * DEPTH) ** 0.25
EPS = 1e-6
D_FF = 2816

GDN_H = 4
GDN_DK = 128
GDN_DV = 128
GDN_CONV = 4
GDN_QKV = GDN_H * (2 * GDN_DK + GDN_DV)

MLA_H = 4
MLA_NOPE = 128
MLA_ROPE = 64
MLA_V = 128
MLA_Q_LORA = 256
MLA_KV_LORA = 128
MLA_SCALE = (MLA_NOPE + MLA_ROPE) ** -0.5
ROPE_THETA = 10000.0
LOG2_E = math.log2(math.e)
MLA_QSCALE = MLA_SCALE * LOG2_E

OFF_Z = GDN_QKV
OFF_B = OFF_Z + GDN_H * GDN_DV
OFF_A = OFF_B + GDN_H
OFF_QA = OFF_A + GDN_H
OFF_KVA = OFF_QA + MLA_Q_LORA

S5_M = 16
S5_G = D_MODEL // S5_M
S5_P = 64
S5_STATE = S5_G * S5_P
S5_KT = 256
S5_NKT = D_MODEL // S5_KT
S5_GPT = S5_KT // S5_M
S5_SPT = S5_GPT * S5_P
S5_SUB = 16

LANE = 128
SUBLANE = 8
VMEM_LIMIT = 56 * 1024 * 1024

FF_CHUNK = 256
FF_NCH = D_FF // FF_CHUNK
FF_SUB = 256
FF_ROWS = 1024
GDN_C = 128
MLA_DIAG = 256

E1_QKV = 0
E1_Z = GDN_QKV
E1_QA = E1_Z + GDN_H * GDN_DV
E1_CKV = E1_QA + MLA_Q_LORA
E1_KRA = E1_CKV + MLA_KV_LORA
E1_KRB = E1_KRA + LANE
E1_BA = E1_KRB + LANE
E1_COLS = E1_BA + LANE


def _cparams(sem):
    return pltpu.CompilerParams(dimension_semantics=sem, vmem_limit_bytes=VMEM_LIMIT)


def _dot(a, b):
    return jnp.dot(a, b, preferred_element_type=F32)


def _dot_nt(a, b):
    return lax.dot_general(a, b, (((1,), (1,)), ((), ())), preferred_element_type=F32)


def _sigmoid(x):
    return 1.0 / (1.0 + jnp.exp(-x))


def _silu(x):
    return x * _sigmoid(x)


def _layer_norm(y, g, b):
    mu = jnp.mean(y, axis=-1, keepdims=True)
    yc = y - mu
    var = jnp.mean(yc * yc, axis=-1, keepdims=True)
    return yc * lax.rsqrt(var + EPS) * g + b


def _rms_norm(y, g):
    return y * lax.rsqrt(jnp.mean(y * y, axis=-1, keepdims=True) + EPS) * g


def _split3(a):
    h = a.astype(BF16)
    r = a - h.astype(F32)
    m = r.astype(BF16)
    l = (r - m.astype(F32)).astype(BF16)
    return h, m, l


def _mm3(a, b):
    n = b.shape[1]
    ah = a.astype(BF16)
    al = (a - ah.astype(F32)).astype(BF16)
    bh = b.astype(BF16)
    bl = (b - bh.astype(F32)).astype(BF16)
    lhs = jnp.concatenate([ah, al], axis=1)
    rhs = jnp.concatenate([jnp.concatenate([bh, bl], axis=1),
                           jnp.concatenate([bh, jnp.zeros_like(bl)], axis=1)], axis=0)
    r = _dot(lhs, rhs)
    return r[:, :n] + r[:, n:]


def _ada_kernel(c_ref, w_ref, b_ref, o_ref):
    sc = _silu(c_ref[...]).astype(BF16)
    o_ref[0] = _dot(sc, w_ref[0].astype(BF16)) + b_ref[0]


def _ada_call(c_all, w_ada, b_ada):
    n = c_all.shape[0]
    ncol = w_ada.shape[-1] // D_MODEL
    return pl.pallas_call(
        _ada_kernel,
        grid=(DEPTH, ncol),
        in_specs=[
            pl.BlockSpec((n, D_MODEL), lambda l, j: (0, 0)),
            pl.BlockSpec((1, D_MODEL, D_MODEL), lambda l, j: (l, 0, j)),
            pl.BlockSpec((1, 1, D_MODEL), lambda l, j: (l, 0, j)),
        ],
        out_specs=pl.BlockSpec((1, n, D_MODEL), lambda l, j: (l, 0, j)),
        out_shape=jax.ShapeDtypeStruct((DEPTH, n, ncol * D_MODEL), F32),
        compiler_params=_cparams(("parallel", "parallel")),
        name="ada_mod",
    )(c_all, w_ada, b_ada.reshape(DEPTH, 1, -1))


def _mod_spec(mod3, layer, sub, part, off, bb):
    base = (layer * (mod3.shape[0] // DEPTH) + off) // bb
    col = sub * 3 + part
    return pl.BlockSpec((bb, 1, D_MODEL), lambda i, t: (base + i, 0, col))


def _ffn_kernel(x_ref, sh_ref, sc_ref, gt_ref, win_ref, wout_ref, lng_ref, lnb_ref, o_ref, ub_sc, h_sc,
                *, sub_rows):
    bb, tt, _ = x_ref.shape
    seg = min(tt, sub_rows)
    segs = [(b, t0) for b in range(bb) for t0 in range(0, tt, seg)]
    per_sub = sub_rows // seg

    for s in range(len(segs) // per_sub):
        rows = slice(s * sub_rows, (s + 1) * sub_rows)
        for k in range(s * per_sub, (s + 1) * per_sub):
            b, t0 = segs[k]
            ub_sc[k * seg:(k + 1) * seg] = (
                x_ref[b, t0:t0 + seg, :] * (1.0 + sc_ref[b]) + sh_ref[b]).astype(BF16)
        for c in range(FF_NCH):
            cols = slice(c * FF_CHUNK, (c + 1) * FF_CHUNK)
            gate = _dot(ub_sc[rows], win_ref[:, c * FF_CHUNK:(c + 1) * FF_CHUNK])
            up = _dot(ub_sc[rows], win_ref[:, D_FF + c * FF_CHUNK:D_FF + (c + 1) * FF_CHUNK])
            h_sc[rows, cols] = (_silu(gate) * up).astype(BF16)
        h = _dot(h_sc[rows], wout_ref[...])
        for j in range(per_sub):
            b, t0 = segs[s * per_sub + j]
            y = DEEP_ALPHA * x_ref[b, t0:t0 + seg, :] + 0.5 * (1.0 + gt_ref[b]) * h[j * seg:(j + 1) * seg]
            o_ref[b, t0:t0 + seg, :] = _layer_norm(y, lng_ref[...], lnb_ref[...])


def _ffn_call(x, mod3, layer, sub, off, win_r, wout_r, lng, lnb, bb, tt):
    nb, T, _ = x.shape
    const2 = lambda i, t: (0, 0)
    rows = bb * tt
    sub_rows = min(FF_SUB, rows)
    assert rows % sub_rows == 0 and (tt % sub_rows == 0 or sub_rows % tt == 0)
    tok = pl.BlockSpec((bb, tt, D_MODEL), lambda i, t: (i, t, 0))
    return pl.pallas_call(
        functools.partial(_ffn_kernel, sub_rows=sub_rows),
        grid=(nb // bb, T // tt),
        in_specs=[
            tok,
            _mod_spec(mod3, layer, sub, 0, off, bb),
            _mod_spec(mod3, layer, sub, 1, off, bb),
            _mod_spec(mod3, layer, sub, 2, off, bb),
            pl.BlockSpec(win_r.shape, const2, pipeline_mode=pl.Buffered(1)),
            pl.BlockSpec(wout_r.shape, const2, pipeline_mode=pl.Buffered(1)),
            pl.BlockSpec((1, D_MODEL), const2),
            pl.BlockSpec((1, D_MODEL), const2),
        ],
        out_specs=tok,
        out_shape=jax.ShapeDtypeStruct(x.shape, F32),
        scratch_shapes=[pltpu.VMEM((rows, D_MODEL), BF16), pltpu.VMEM((rows, D_FF), BF16)],
        compiler_params=_cparams(("parallel", "parallel")),
        name="ffn",
    )(x, mod3, mod3, mod3, win_r, wout_r, lng, lnb)


def _e1_kernel(x_ref, sh_ref, sc_ref, w_ref, conv0_ref, convw_ref, cos_ref, sin_ref, cost_ref, sint_ref,
               qn_ref, kvn_ref, wqa_ref, wqb_ref, wkn_ref, wv_ref,
               qkvn_ref, z_ref, bg_ref, qa_ref, ka_ref, vv_ref, ckv_ref, kr_ref, tail_ref, tail_sc,
               *, feature_major):
    bb, tt, _ = x_ref.shape
    rows = bb * tt
    assert bb == 1 or not feature_major

    @pl.when(pl.program_id(1) == 0)
    def _():
        tail_sc[...] = conv0_ref[...]

    u = x_ref[...] * (1.0 + sc_ref[...]) + sh_ref[...]
    ub = u.reshape(rows, D_MODEL).astype(BF16)

    pre = _dot(ub, w_ref[:, E1_QKV:E1_Z])
    for b in range(bb):
        pre_b = pre[b * tt:(b + 1) * tt]
        hist = jnp.concatenate([tail_sc[b], pre_b], axis=0)
        y = pre_b * convw_ref[GDN_CONV - 1:GDN_CONV, :]
        for lag in range(1, GDN_CONV):
            w_row = convw_ref[GDN_CONV - 1 - lag:GDN_CONV - lag, :]
            y = y + pltpu.roll(hist, lag, 0)[SUBLANE:] * w_row
        tail_sc[b] = pre_b[tt - SUBLANE:]
        y = _silu(y)
        for h in range(GDN_H):
            qh = y[:, h * GDN_DK:(h + 1) * GDN_DK]
            qkvn_ref[b, :, h * GDN_DK:(h + 1) * GDN_DK] = (
                qh * lax.rsqrt(jnp.sum(qh * qh, axis=-1, keepdims=True) + EPS) * (GDN_DK ** -0.5))
            ko = GDN_H * GDN_DK + h * GDN_DK
            kh = y[:, ko:ko + GDN_DK]
            qkvn_ref[b, :, ko:ko + GDN_DK] = kh * lax.rsqrt(jnp.sum(kh * kh, axis=-1, keepdims=True) + EPS)
        vo = 2 * GDN_H * GDN_DK
        qkvn_ref[b, :, vo:] = y[:, vo:]
    tail_ref[...] = tail_sc[...]

    z_ref[...] = _dot(ub, w_ref[:, E1_Z:E1_QA]).reshape(bb, tt, -1)
    bg_ref[...] = _dot(ub, w_ref[:, E1_BA:E1_COLS]).reshape(bb, tt, -1)

    cos = cos_ref[...]
    sin = sin_ref[...]
    if bb > 1:
        cos = jnp.concatenate([cos] * bb, axis=0)
        sin = jnp.concatenate([sin] * bb, axis=0)

    cq = _rms_norm(_dot(ub, w_ref[:, E1_QA:E1_CKV]), qn_ref[...])
    if feature_major:
        cq_t = cq.T.astype(BF16)
        qa = _dot(wqa_ref[...], cq_t)
        qb = _dot(wqb_ref[...], cq_t)
        for h in range(MLA_H):
            o = 2 * LANE * h
            qa_ref[0, o:o + LANE, :] = (qa[o:o + LANE] * MLA_QSCALE).astype(BF16)
            rot = qa[o + LANE:o + 2 * LANE] * cost_ref[...] + qb[h * LANE:(h + 1) * LANE] * sint_ref[...]
            qa_ref[0, o + LANE:o + 2 * LANE, :] = (rot * MLA_QSCALE).astype(BF16)
    else:
        cqb = cq.astype(BF16)
        qa = _dot(cqb, wqa_ref[...])
        qb = _dot(cqb, wqb_ref[...])
        for h in range(MLA_H):
            o = 2 * LANE * h
            qa_ref[:, :, o:o + LANE] = (qa[:, o:o + LANE] * MLA_QSCALE).astype(BF16).reshape(bb, tt, LANE)
            rot = qa[:, o + LANE:o + 2 * LANE] * cos + qb[:, h * LANE:(h + 1) * LANE] * sin
            qa_ref[:, :, o + LANE:o + 2 * LANE] = (rot * MLA_QSCALE).astype(BF16).reshape(bb, tt, LANE)

    ckv = _rms_norm(_dot(ub, w_ref[:, E1_CKV:E1_KRA]), kvn_ref[...])
    ckv_ref[...] = ckv.reshape(bb, tt, -1)
    kr = _dot(ub, w_ref[:, E1_KRA:E1_KRB]) * cos + _dot(ub, w_ref[:, E1_KRB:E1_BA]) * sin
    kr_ref[...] = kr[:, :MLA_ROPE].reshape(bb, tt, -1)
    cb = ckv.astype(BF16)
    krb = kr.astype(BF16).reshape(bb, tt, LANE)
    kn = _dot(cb, wkn_ref[...])
    for h in range(MLA_H):
        o = 2 * LANE * h
        ka_ref[:, :, o:o + LANE] = kn[:, h * LANE:(h + 1) * LANE].astype(BF16).reshape(bb, tt, LANE)
        ka_ref[:, :, o + LANE:o + 2 * LANE] = krb
    if feature_major:
        vv_ref[0] = _dot(wv_ref[...], ckv.T.astype(BF16)).astype(BF16)
    else:
        vv_ref[...] = _dot(cb, wv_ref[...]).astype(BF16).reshape(bb, tt, -1)


def _e1_call(x, mod3, layer, off, ew, conv0p, rope, bb, tt, feature_major):
    nb, T, _ = x.shape
    tok = lambda i, t: (i, t, 0)
    const2 = lambda i, t: (0, 0)

    def full(a):
        return pl.BlockSpec(a.shape, const2)

    out_dims = [(GDN_QKV, F32), (GDN_H * GDN_DV, F32), (LANE, F32), (MLA_H * 2 * LANE, BF16),
                (MLA_H * 2 * LANE, BF16), (MLA_H * MLA_V, BF16), (MLA_KV_LORA, F32), (MLA_ROPE, F32)]
    out_shape = [jax.ShapeDtypeStruct((nb, T, d), dt) for d, dt in out_dims]
    out_specs = [pl.BlockSpec((bb, tt, d), tok) for d, _ in out_dims]
    wqa, wqb, wv = ew["wqa"], ew["wqb"], ew["wv"]
    if feature_major:
        wqa, wqb, wv = ew["wqa_t"], ew["wqb_t"], ew["wv_t"]
        for idx in (3, 5):
            d = out_dims[idx][0]
            out_shape[idx] = jax.ShapeDtypeStruct((nb, d, T), BF16)
            out_specs[idx] = pl.BlockSpec((bb, d, tt), lambda i, t: (i, 0, t))
    out_shape.append(jax.ShapeDtypeStruct((nb, SUBLANE, GDN_QKV), F32))
    out_specs.append(pl.BlockSpec((bb, SUBLANE, GDN_QKV), lambda i, t: (i, 0, 0)))
    cos_t, sin_t = rope["rows"]
    cos_f, sin_f = rope["cols"]
    return pl.pallas_call(
        functools.partial(_e1_kernel, feature_major=feature_major),
        grid=(nb // bb, T // tt),
        in_specs=[
            pl.BlockSpec((bb, tt, D_MODEL), tok),
            _mod_spec(mod3, layer, 1, 0, off, bb),
            _mod_spec(mod3, layer, 1, 1, off, bb),
            full(ew["w_e1"]),
            pl.BlockSpec((bb, SUBLANE, GDN_QKV), lambda i, t: (i, 0, 0)),
            full(ew["conv_w"]),
            pl.BlockSpec((tt, LANE), lambda i, t: (t, 0)),
            pl.BlockSpec((tt, LANE), lambda i, t: (t, 0)),
            pl.BlockSpec((LANE, tt), lambda i, t: (0, t)),
            pl.BlockSpec((LANE, tt), lambda i, t: (0, t)),
            full(ew["q_norm"]), full(ew["kv_norm"]),
            full(wqa), full(wqb), full(ew["wkn"]), full(wv),
        ],
        out_specs=out_specs,
        out_shape=out_shape,
        scratch_shapes=[pltpu.VMEM((bb, SUBLANE, GDN_QKV), F32)],
        compiler_params=_cparams(("parallel", "arbitrary")),
        name="mix_proj",
    )(x, mod3, mod3, ew["w_e1"], conv0p, ew["conv_w"], cos_t, sin_t, cos_f, sin_f, ew["q_norm"], ew["kv_norm"],
      wqa, wqb, ew["wkn"], wv)


def _gdn_kernel(qkvn_ref, z_ref, bg_ref, s0_ref, alog_ref, dtb_ref, nw_ref, o_ref, s_ref, s_sc, *, n_valid):
    bb = qkvn_ref.shape[0]
    C = GDN_C
    units = [(b, h) for b in range(bb) for h in range(GDN_H)]

    @pl.when(pl.program_id(1) == 0)
    def _():
        s_sc[...] = s0_ref[...]

    row = lax.broadcasted_iota(jnp.int32, (C, C), 0)
    col = lax.broadcasted_iota(jnp.int32, (C, C), 1)
    incl = row >= col
    strict = row > col
    tri = jnp.where(incl, 1.0, 0.0).astype(BF16)
    eye = jnp.where(row == col, 1.0, 0.0)

    beta_all, G_all, G_t = [], [], []
    for b in range(bb):
        bg = bg_ref[b]
        beta_all.append(_sigmoid(bg))
        xs = bg + dtb_ref[...]
        softplus = jnp.maximum(xs, 0.0) + jnp.log(1.0 + jnp.exp(-jnp.abs(xs)))
        g_all = -jnp.exp(alog_ref[...]) * softplus
        if n_valid < C:
            g_all = jnp.where(row < n_valid, g_all, 0.0)
        g3 = _dot(tri, jnp.concatenate(_split3(g_all), axis=1))
        G_all.append(g3[:, :LANE] + g3[:, LANE:2 * LANE] + g3[:, 2 * LANE:])
        G_t.append(G_all[b].T)

    def load(b, h, part):
        o = (part * GDN_H + h) * GDN_DK
        return qkvn_ref[b, :, o:o + GDN_DK]

    q = [load(b, h, 0) for b, h in units]
    k = [load(b, h, 1) for b, h in units]
    v = [load(b, h, 2) for b, h in units]
    kb = [x.astype(BF16) for x in k]
    beta = [beta_all[b][:, h:h + 1] for b, h in units]
    Gc = [G_all[b][:, GDN_H + h:GDN_H + h + 1] for b, h in units]
    Gr = [G_t[b][GDN_H + h:GDN_H + h + 1, :] for b, h in units]
    decay = [jnp.where(incl, jnp.exp(jnp.where(incl, c - r, 0.0)), 0.0) for c, r in zip(Gc, Gr)]
    kk = [_dot_nt(x, x) for x in kb]
    lower = [jnp.where(strict, bt * d * m, 0.0) for bt, d, m in zip(beta, decay, kk)]
    inv = [eye - m for m in lower]
    pw = [_mm3(m, m) for m in lower]
    n_sq = int(math.log2(C)) - 1
    for it in range(n_sq):
        inv = [a + _mm3(a, p) for a, p in zip(inv, pw)]
        if it + 1 < n_sq:
            pw = [_mm3(p, p) for p in pw]
    gam = [jnp.exp(c) for c in Gc]
    sol_v = [_mm3(a, bt * x) for a, bt, x in zip(inv, beta, v)]
    sol_k = [_mm3(a, (bt * g) * x).astype(BF16) for a, bt, g, x in zip(inv, beta, gam, k)]
    attn = [(_dot_nt(x.astype(BF16), y) * d).astype(BF16) for x, y, d in zip(q, kb, decay)]
    qg = [(g * x).astype(BF16) for g, x in zip(gam, q)]
    g_end = [c[C - 1:C, :] for c in Gc]
    kd = [(jnp.exp(e - c) * x).T.astype(BF16) for e, c, x in zip(g_end, Gc, k)]

    s0 = [s_sc[b, h] for b, h in units]
    s0b = [x.astype(BF16) for x in s0]
    ub = [(a - _dot(w, s)).astype(BF16) for a, w, s in zip(sol_v, sol_k, s0b)]
    o = [_dot(x, s) + _dot(a, y) for x, s, a, y in zip(qg, s0b, attn, ub)]
    s1 = [jnp.exp(e) * s + _dot(x, y) for e, s, x, y in zip(g_end, s0, kd, ub)]
    for i, (b, h) in enumerate(units):
        s_sc[b, h] = s1[i]
        zh = z_ref[b, :, h * GDN_DV:(h + 1) * GDN_DV]
        o_ref[b, :, h * GDN_DV:(h + 1) * GDN_DV] = (_rms_norm(o[i], nw_ref[...]) * _silu(zh)).astype(BF16)
    s_ref[...] = s_sc[...]


def _gdn_call(qkvn, z, bg, s0, ew, bb, n_valid):
    nb, T, _ = qkvn.shape
    tok = lambda i, c: (i, c, 0)
    const2 = lambda i, c: (0, 0)
    st = lambda i, c: (i, 0, 0, 0)
    return pl.pallas_call(
        functools.partial(_gdn_kernel, n_valid=n_valid),
        grid=(nb // bb, T // GDN_C),
        in_specs=[
            pl.BlockSpec((bb, GDN_C, GDN_QKV), tok),
            pl.BlockSpec((bb, GDN_C, GDN_H * GDN_DV), tok),
            pl.BlockSpec((bb, GDN_C, LANE), tok),
            pl.BlockSpec((bb, GDN_H, GDN_DK, GDN_DV), st),
            pl.BlockSpec((1, LANE), const2),
            pl.BlockSpec((1, LANE), const2),
            pl.BlockSpec((1, GDN_DV), const2),
        ],
        out_specs=[pl.BlockSpec((bb, GDN_C, GDN_H * GDN_DV), tok),
                   pl.BlockSpec((bb, GDN_H, GDN_DK, GDN_DV), st)],
        out_shape=[jax.ShapeDtypeStruct((nb, T, GDN_H * GDN_DV), BF16),
                   jax.ShapeDtypeStruct((nb, GDN_H, GDN_DK, GDN_DV), F32)],
        scratch_shapes=[pltpu.VMEM((bb, GDN_H, GDN_DK, GDN_DV), F32)],
        compiler_params=_cparams(("parallel", "arbitrary")),
        name="gdn",
    )(qkvn, z, bg, s0, ew["alog"], ew["dtb"], ew["norm_w"])


def _attn_init(m_sc, l_sc, acc_sc):
    m_sc[...] = jnp.full_like(m_sc, -jnp.inf)
    l_sc[...] = jnp.zeros_like(l_sc)
    acc_sc[...] = jnp.zeros_like(acc_sc)


def _mix_out(og, heads_out, x_ref, gt_ref, wout_ref, lng_ref, lnb_ref, o_ref):
    mix = _dot(jnp.concatenate([og] + heads_out, axis=-1), wout_ref[...])
    y = DEEP_ALPHA * x_ref[0] + (1.0 + gt_ref[0]) * mix
    o_ref[0] = _layer_norm(y, lng_ref[...], lnb_ref[...])


def _mla_prompt_kernel(qa_ref, ka_ref, vv_ref, og_ref, x_ref, gt_ref, wout_ref, lng_ref, lnb_ref, o_ref,
                       m_sc, l_sc, acc_sc):
    tq = qa_ref.shape[2]
    i = pl.program_id(1)
    _attn_init(m_sc, l_sc, acc_sc)
    heads = range(MLA_H)
    def block(ks, nk, q0, nq, q_off):
        qs = slice(q0, q0 + nq)
        s = [_dot(ka_ref[0, pl.ds(ks, nk), 2 * LANE * h:2 * LANE * (h + 1)],
                  qa_ref[0, 2 * LANE * h:2 * LANE * (h + 1), qs]) for h in heads]
        if q_off is not None:
            vis = ((q_off + lax.broadcasted_iota(jnp.int32, (nk, nq), 1)) // CHUNK
                   >= lax.broadcasted_iota(jnp.int32, (nk, nq), 0) // CHUNK)
            s = [jnp.where(vis, x, -jnp.inf) for x in s]
        m_prev = [m_sc[h, :, qs] for h in heads]
        m_new = [jnp.maximum(m, jnp.max(x, axis=0, keepdims=True)) for m, x in zip(m_prev, s)]
        p = [jnp.exp2(x - m) for x, m in zip(s, m_new)]
        alpha = [jnp.exp2(a - b) for a, b in zip(m_prev, m_new)]
        pv = [_dot(vv_ref[0, MLA_V * h:MLA_V * (h + 1), pl.ds(ks, nk)], p[h].astype(BF16)) for h in heads]
        for h in heads:
            l_sc[h, :, qs] = alpha[h] * l_sc[h, :, qs] + jnp.sum(p[h], axis=0, keepdims=True)
            acc_sc[h, :, qs] = alpha[h] * acc_sc[h, :, qs] + pv[h]
            m_sc[h, :, qs] = m_new[h]

    def body(j, carry):
        block(pl.multiple_of(j * tq, tq), tq, 0, tq, None)
        return carry

    lax.fori_loop(0, i, body, 0)
    sub = min(tq, MLA_DIAG)
    d0 = pl.multiple_of(i * tq, tq)
    for kb in range(tq // sub):
        ks = pl.multiple_of(d0 + kb * sub, sub)
        block(ks, sub, kb * sub, sub, 0)
        if (kb + 1) * sub < tq:
            block(ks, sub, (kb + 1) * sub, tq - (kb + 1) * sub, None)
    heads_out = [(acc_sc[h] / l_sc[h]).T.astype(BF16) for h in heads]
    _mix_out(og_ref[0], heads_out, x_ref, gt_ref, wout_ref, lng_ref, lnb_ref, o_ref)


def _mla_sample_kernel(qa_ref, pckv_ref, pkr_ref, ka_ref, vv_ref, og_ref, x_ref, gt_ref, wknt_ref, wv_ref,
                       wout_ref, lng_ref, lnb_ref, o_ref):
    ckv = pckv_ref[0]
    kn_t = _dot(wknt_ref[...], ckv.T.astype(BF16)).astype(BF16)
    kr_t = pkr_ref[0].T.astype(BF16)
    vp = _dot(ckv.astype(BF16), wv_ref[...]).astype(BF16)
    heads_out = []
    for h in range(MLA_H):
        q = qa_ref[0, :, 2 * LANE * h:2 * LANE * (h + 1)]
        k_t = jnp.concatenate([kn_t[h * MLA_NOPE:(h + 1) * MLA_NOPE], kr_t], axis=0)
        s_p = _dot(q, k_t)
        s_n = _dot_nt(q, ka_ref[0, :, 2 * LANE * h:2 * LANE * (h + 1)])
        m = jnp.maximum(jnp.max(s_p, axis=-1, keepdims=True), jnp.max(s_n, axis=-1, keepdims=True))
        p_p = jnp.exp2(s_p - m)
        p_n = jnp.exp2(s_n - m)
        l = jnp.sum(p_p, axis=-1, keepdims=True) + jnp.sum(p_n, axis=-1, keepdims=True)
        o = (_dot(p_p.astype(BF16), vp[:, MLA_V * h:MLA_V * (h + 1)])
             + _dot(p_n.astype(BF16), vv_ref[0, :, MLA_V * h:MLA_V * (h + 1)]))
        heads_out.append((o / l).astype(BF16))
    _mix_out(og_ref[0], heads_out, x_ref, gt_ref, wout_ref, lng_ref, lnb_ref, o_ref)


def _mla_prompt_call(qa, ka, vv, og, x, mod3, layer, off, w_out, lng, lnb, tq):
    nb, T, _ = x.shape
    tok = lambda b, i: (b, i, 0)
    seq = lambda b, i: (b, 0, 0)
    const2 = lambda b, i: (0, 0)
    return pl.pallas_call(
        _mla_prompt_kernel,
        grid=(nb, T // tq),
        in_specs=[
            pl.BlockSpec((1, qa.shape[1], tq), lambda b, i: (b, 0, i)),
            pl.BlockSpec((1,) + ka.shape[1:], seq),
            pl.BlockSpec((1,) + vv.shape[1:], seq),
            pl.BlockSpec((1, tq, og.shape[-1]), tok),
            pl.BlockSpec((1, tq, D_MODEL), tok),
            _mod_spec(mod3, layer, 1, 2, off, 1),
            pl.BlockSpec(w_out.shape, const2),
            pl.BlockSpec((1, D_MODEL), const2),
            pl.BlockSpec((1, D_MODEL), const2),
        ],
        out_specs=pl.BlockSpec((1, tq, D_MODEL), tok),
        out_shape=jax.ShapeDtypeStruct(x.shape, F32),
        scratch_shapes=[pltpu.VMEM((MLA_H, 1, tq), F32), pltpu.VMEM((MLA_H, 1, tq), F32),
                        pltpu.VMEM((MLA_H, MLA_V, tq), F32)],
        compiler_params=_cparams(("parallel", "arbitrary")),
        name="mla_prompt",
    )(qa, ka, vv, og, x, mod3, w_out, lng, lnb)


def _mla_sample_call(qa, past_ckv, past_kr128, ka, vv, og, x, mod3, layer, off, ew, lng, lnb):
    nb, T, _ = x.shape
    seq = lambda b, i: (b, 0, 0)
    const2 = lambda b, i: (0, 0)

    def whole(a):
        return pl.BlockSpec((1,) + a.shape[1:], seq)

    def full(a):
        return pl.BlockSpec(a.shape, const2)

    return pl.pallas_call(
        _mla_sample_kernel,
        grid=(nb, 1),
        in_specs=[whole(qa), whole(past_ckv), whole(past_kr128), whole(ka), whole(vv), whole(og), whole(x),
                  _mod_spec(mod3, layer, 1, 2, off, 1),
                  full(ew["wkn_t"]), full(ew["wv"]), full(ew["w_out"]), full(lng), full(lnb)],
        out_specs=whole(x),
        out_shape=jax.ShapeDtypeStruct(x.shape, F32),
        compiler_params=_cparams(("parallel", "arbitrary")),
        name="mla_sample",
    )(qa, past_ckv, past_kr128, ka, vv, og, x, mod3, ew["wkn_t"], ew["wv"], ew["w_out"], lng, lnb)


def _s5prep_kernel(lre_ref, lim_ref, ldt_ref, br_ref, bi_ref, abre_ref, abim_ref, bbre_ref, bbim_ref):
    dt = jnp.exp(ldt_ref[0])
    lr = jnp.minimum(lre_ref[0], -1e-4)
    li = lim_ref[0]
    mag = jnp.exp(lr * dt)
    ab_re = mag * jnp.cos(li * dt)
    ab_im = mag * jnp.sin(li * dt)
    den = lr * lr + li * li
    coef_re = ((ab_re - 1.0) * lr + ab_im * li) / den
    coef_im = (ab_im * lr - (ab_re - 1.0) * li) / den
    br = br_ref[0]
    bi = bi_ref[0]
    abre_ref[0] = ab_re
    abim_ref[0] = ab_im
    bbre_ref[0] = coef_re * br - coef_im * bi
    bbim_ref[0] = coef_re * bi + coef_im * br


def _s5prep_call(lre_x, lim_x, ldt_x, br_t, bi_t):
    n = lre_x.shape[0]
    blk = pl.BlockSpec((1, D_MODEL, S5_P), lambda i: (i, 0, 0))
    return pl.pallas_call(
        _s5prep_kernel,
        grid=(n,),
        in_specs=[blk, blk, pl.BlockSpec((1, D_MODEL, 1), lambda i: (i, 0, 0)), blk, blk],
        out_specs=[blk] * 4,
        out_shape=[jax.ShapeDtypeStruct((n, D_MODEL, S5_P), F32)] * 4,
        compiler_params=_cparams(("parallel",)),
        name="s5_discretise",
    )(lre_x, lim_x, ldt_x, br_t, bi_t)


def _s5_kernel(x_ref, sh_ref, sc_ref, gt_ref, h0re_ref, h0im_ref, abre_ref, abim_ref, bb_ref, cre_ref, cim_ref,
               dskip_ref, wglu_ref, wout_ref, lng_ref, lnb_ref, o_ref, hre_ref, him_ref,
               hre_sc, him_sc, xt_sc, ub_sc, st_sc, y_sc):
    nb, tt, _ = x_ref.shape
    n_sub, ts = xt_sc.shape[0], xt_sc.shape[1]
    rows = ts * nb
    half = S5_SPT // 2

    @pl.when(pl.program_id(0) == 0)
    def _():
        hre_sc[...] = h0re_ref[...]
        him_sc[...] = h0im_ref[...]

    for s in range(n_sub):
        frames = slice(s * ts, (s + 1) * ts)
        xt_sc[s] = pltpu.einshape("btd->tbd", x_ref[:, frames, :])

        def modulated():
            return (xt_sc[s] * (1.0 + sc_ref[...]) + sh_ref[...]).reshape(rows, D_MODEL)

        ub_sc[s] = modulated().astype(BF16)
        for kt in range(S5_NKT):
            st_sc[s, kt] = _dot(ub_sc[s, :, kt * S5_KT:(kt + 1) * S5_KT], bb_ref[kt])
        for kt in range(S5_NKT):
            for lc in range(2):
                so = kt * S5_SPT + lc * half
                re_cols = slice(lc * half, (lc + 1) * half)
                im_cols = slice(S5_SPT + lc * half, S5_SPT + (lc + 1) * half)
                a_re = jnp.broadcast_to(abre_ref[:, so:so + half], (nb, half))
                a_im = jnp.broadcast_to(abim_ref[:, so:so + half], (nb, half))
                h_re = hre_sc[:, so:so + half]
                h_im = him_sc[:, so:so + half]
                for t in range(ts):
                    frame = slice(t * nb, (t + 1) * nb)
                    n_re = a_re * h_re - a_im * h_im + st_sc[s, kt, frame, re_cols]
                    n_im = a_re * h_im + a_im * h_re + st_sc[s, kt, frame, im_cols]
                    st_sc[s, kt, frame, re_cols] = n_re
                    st_sc[s, kt, frame, im_cols] = n_im
                    h_re, h_im = n_re, n_im
                hre_sc[:, so:so + half] = h_re
                him_sc[:, so:so + half] = h_im
            y_sc[s, :, kt * S5_KT:(kt + 1) * S5_KT] = (
                _dot(st_sc[s, kt, :, :S5_SPT].astype(BF16), cre_ref[kt])
                - _dot(st_sc[s, kt, :, S5_SPT:].astype(BF16), cim_ref[kt]))

        y = y_sc[s] + dskip_ref[...] * modulated()
        yg = 0.5 * y * (1.0 + jnp.tanh(math.sqrt(2.0 / math.pi) * (y + 0.044715 * (y * y * y))))
        glu = _dot(yg.astype(BF16), wglu_ref[...])
        mix = _dot((yg * _sigmoid(glu)).astype(BF16), wout_ref[...])
        out = DEEP_ALPHA * xt_sc[s] + (1.0 + gt_ref[...]) * mix.reshape(ts, nb, D_MODEL)
        o_ref[:, frames, :] = pltpu.einshape("tbd->btd", _layer_norm(out, lng_ref[...], lnb_ref[...]))

    @pl.when(pl.program_id(0) == pl.num_programs(0) - 1)
    def _():
        hre_ref[...] = hre_sc[...]
        him_ref[...] = him_sc[...]


def _s5_call(x, mod, layer, stream, h0re, h0im, sw, lng, lnb, tt):
    nb, T, _ = x.shape
    ts = min(tt, S5_SUB)
    n_sub = tt // ts
    rows = ts * nb
    const2 = lambda t: (0, 0)
    const3 = lambda t: (0, 0, 0)

    def mspec(part):
        return pl.BlockSpec((1, nb, D_MODEL), lambda t: (layer, stream, 3 + part))

    def full(a):
        return pl.BlockSpec(a.shape, const2 if a.ndim == 2 else const3, pipeline_mode=pl.Buffered(1))

    return pl.pallas_call(
        _s5_kernel,
        grid=(T // tt,),
        in_specs=[
            pl.BlockSpec((nb, tt, D_MODEL), lambda t: (0, t, 0)),
            mspec(0), mspec(1), mspec(2),
            full(h0re), full(h0im), full(sw["ab_re"]), full(sw["ab_im"]),
            full(sw["bb"]), full(sw["c_re"]), full(sw["c_im"]), full(sw["d_skip"]),
            full(sw["w_glu"]), full(sw["w_out"]), full(lng), full(lnb),
        ],
        out_specs=[pl.BlockSpec((nb, tt, D_MODEL), lambda t: (0, t, 0)),
                   pl.BlockSpec((nb, S5_STATE), const2), pl.BlockSpec((nb, S5_STATE), const2)],
        out_shape=[jax.ShapeDtypeStruct(x.shape, F32),
                   jax.ShapeDtypeStruct((nb, S5_STATE), F32), jax.ShapeDtypeStruct((nb, S5_STATE), F32)],
        scratch_shapes=[pltpu.VMEM((nb, S5_STATE), F32), pltpu.VMEM((nb, S5_STATE), F32),
                        pltpu.VMEM((n_sub, ts, nb, D_MODEL), F32), pltpu.VMEM((n_sub, rows, D_MODEL), BF16),
                        pltpu.VMEM((n_sub, S5_NKT, rows, 2 * S5_SPT), F32),
                        pltpu.VMEM((n_sub, rows, D_MODEL), F32)],
        compiler_params=_cparams(("arbitrary",)),
        name="s5_mixer",
    )(x, mod, mod, mod, h0re, h0im, sw["ab_re"], sw["ab_im"], sw["bb"], sw["c_re"], sw["c_im"],
      sw["d_skip"], sw["w_glu"], sw["w_out"], lng, lnb)


def _prep_ffn(w_in, w_out):
    return w_in.astype(BF16), w_out.astype(BF16)


def _prep_even(w_in, conv_w, a_log, dt_bias, norm_w, q_norm, w_q_b, kv_norm, w_kv_b, w_out):
    half = MLA_ROPE // 2
    zcol = lambda n: jnp.zeros((w_in.shape[0], n), w_in.dtype)
    k1 = w_in[:, OFF_KVA + MLA_KV_LORA:OFF_KVA + MLA_KV_LORA + half]
    k2 = w_in[:, OFF_KVA + MLA_KV_LORA + half:OFF_KVA + MLA_KV_LORA + MLA_ROPE]
    w_e1 = jnp.concatenate([
        w_in[:, :OFF_B],
        w_in[:, OFF_QA:OFF_KVA + MLA_KV_LORA],
        k1, k2, zcol(LANE - MLA_ROPE),
        -k2, k1, zcol(LANE - MLA_ROPE),
        w_in[:, OFF_B:OFF_QA], zcol(LANE - 2 * GDN_H),
    ], axis=1).astype(BF16)
    zq = jnp.zeros((MLA_Q_LORA, LANE - MLA_ROPE), w_q_b.dtype)
    qa_cols, qb_cols = [], []
    for h in range(MLA_H):
        o = h * (MLA_NOPE + MLA_ROPE)
        r1 = w_q_b[:, o + MLA_NOPE:o + MLA_NOPE + half]
        r2 = w_q_b[:, o + MLA_NOPE + half:o + MLA_NOPE + MLA_ROPE]
        qa_cols += [w_q_b[:, o:o + MLA_NOPE], r1, r2, zq]
        qb_cols += [-r2, r1, zq]
    kv = w_kv_b.reshape(MLA_KV_LORA, MLA_H, MLA_NOPE + MLA_V)
    pad = lambda v: jnp.pad(v.astype(F32), (GDN_H, LANE - 2 * GDN_H)).reshape(1, LANE)
    return {
        "w_e1": w_e1,
        "conv_w": conv_w.astype(F32),
        "alog": pad(a_log), "dtb": pad(dt_bias),
        "norm_w": norm_w.reshape(1, -1).astype(F32),
        "q_norm": q_norm.reshape(1, -1).astype(F32),
        "kv_norm": kv_norm.reshape(1, -1).astype(F32),
        "wqa": jnp.concatenate(qa_cols, axis=1).astype(BF16),
        "wqb": jnp.concatenate(qb_cols, axis=1).astype(BF16),
        "wqa_t": jnp.concatenate(qa_cols, axis=1).T.astype(BF16),
        "wqb_t": jnp.concatenate(qb_cols, axis=1).T.astype(BF16),
        "wv_t": kv[:, :, MLA_NOPE:].reshape(MLA_KV_LORA, -1).T.astype(BF16),
        "wkn": kv[:, :, :MLA_NOPE].reshape(MLA_KV_LORA, -1).astype(BF16),
        "wkn_t": kv[:, :, :MLA_NOPE].reshape(MLA_KV_LORA, -1).T.astype(BF16),
        "wv": kv[:, :, MLA_NOPE:].reshape(MLA_KV_LORA, -1).astype(BF16),
        "w_out": w_out.astype(BF16),
    }


def _rope_tables(pos):
    half = MLA_ROPE // 2
    inv = ROPE_THETA ** (-jnp.arange(half, dtype=F32) / half)
    ang = pos.astype(F32)[:, None] * inv[None, :]
    z = jnp.zeros((pos.shape[0], LANE - MLA_ROPE), F32)
    cos, sin = jnp.cos(ang), jnp.sin(ang)
    cos, sin = jnp.concatenate([cos, cos, z], axis=1), jnp.concatenate([sin, sin, z], axis=1)
    return {"rows": (cos, sin), "cols": (cos.T, sin.T)}


def _prep_s5(lam_re, lam_im, log_dt, b_re, b_im, c_re, c_im, d_skip, w_glu, w_out):
    n = lam_re.shape[0]
    rep = lambda a: jnp.repeat(a.astype(F32), S5_M, axis=1)
    to_gmp = lambda a: jnp.transpose(a.astype(F32), (0, 1, 3, 2)).reshape(n, D_MODEL, S5_P)
    ab_re, ab_im, bb_re, bb_im = _s5prep_call(
        rep(lam_re), rep(lam_im), jnp.repeat(log_dt.astype(F32), S5_M, axis=1)[..., None],
        to_gmp(b_re), to_gmp(b_im))
    eye = jnp.eye(S5_GPT, dtype=F32)

    def embed_in(bb):
        t = bb.reshape(n, S5_NKT, S5_GPT, S5_M, S5_P)
        return jnp.einsum("nkgmp,gh->nkgmhp", t, eye).reshape(n, S5_NKT, S5_KT, S5_SPT)

    def embed_out(c):
        t = c.astype(F32).reshape(n, S5_NKT, S5_GPT, S5_M, S5_P)
        return jnp.einsum("nkgmp,gh->nkhpgm", t, eye).reshape(n, S5_NKT, S5_SPT, S5_KT)

    pick = lambda a: a.reshape(n, S5_G, S5_M, S5_P)[:, :, 0, :].reshape(n, 1, S5_STATE)
    return {
        "ab_re": pick(ab_re), "ab_im": pick(ab_im),
        "bb": jnp.concatenate([embed_in(bb_re), embed_in(bb_im)], axis=-1).astype(BF16),
        "c_re": embed_out(c_re).astype(BF16), "c_im": embed_out(c_im).astype(BF16),
        "d_skip": d_skip.reshape(n, 1, D_MODEL).astype(F32),
        "w_glu": w_glu.astype(BF16), "w_out": w_out.astype(BF16),
    }


def _pad_rows(a, n):
    return jnp.pad(a, ((0, 0), (0, n - a.shape[1]), (0, 0)))


def _run_trunk(x, off, pos, mod, mod3, ffw, evw, s5w, ln_g, ln_b, gdn0, conv0, s5re0, s5im0, past, cfg):
    nb, T, _ = x.shape
    rope = _rope_tables(pos)
    new_ckv, new_kr, new_gdn, new_conv, new_re, new_im = [], [], [], [], [], []
    stream = off // nb
    for l in range(DEPTH):
        lng = lambda s: ln_g[l, s].reshape(1, D_MODEL)
        lnb = lambda s: ln_b[l, s].reshape(1, D_MODEL)
        x = _ffn_call(x, mod3, l, 0, off, *ffw[l][0], lng(0), lnb(0), cfg["ffn_bb"], cfg["ffn_tt"])
        i = l // 2
        if l % 2 == 0:
            ew = evw[i]
            conv0p = jnp.pad(conv0[i].astype(F32), ((0, 0), (SUBLANE - (GDN_CONV - 1), 0), (0, 0)))
            qkvn, z, bg, qa, ka, vv, ckv, kr, tail = _e1_call(
                x, mod3, l, off, ew, conv0p, rope, cfg["e1_bb"], cfg["e1_tt"], past is None)
            n_valid = min(T, GDN_C)
            if T < GDN_C:
                qkvn, z, bg = (_pad_rows(a, GDN_C) for a in (qkvn, z, bg))
            og, s_new = _gdn_call(qkvn, z, bg, gdn0[i].astype(F32), ew, cfg["gdn_bb"], n_valid)
            og = og[:, :T]
            if past is None:
                x = _mla_prompt_call(qa, ka, vv, og, x, mod3, l, off, ew["w_out"], lng(1), lnb(1), cfg["mla_tq"])
            else:
                x = _mla_sample_call(qa, past[0][i], past[1][i], ka, vv, og, x, mod3, l, off, ew, lng(1), lnb(1))
            new_gdn.append(s_new)
            new_conv.append(tail[:, SUBLANE - (GDN_CONV - 1):])
            new_ckv.append(ckv)
            new_kr.append(kr)
        else:
            sw = {k: v[i] for k, v in s5w.items()}
            x, h_re, h_im = _s5_call(x, mod, l, stream, s5re0[i].reshape(nb, S5_STATE).astype(F32),
                                     s5im0[i].reshape(nb, S5_STATE).astype(F32), sw, lng(1), lnb(1), cfg["s5_tt"])
            new_re.append(h_re.reshape(nb, S5_G, S5_P))
            new_im.append(h_im.reshape(nb, S5_G, S5_P))
        x = _ffn_call(x, mod3, l, 2, off, *ffw[l][1], lng(2), lnb(2), cfg["ffn_bb"], cfg["ffn_tt"])
    return (x, jnp.stack(new_ckv), jnp.stack(new_kr), jnp.stack(new_gdn), jnp.stack(new_conv),
            jnp.stack(new_re), jnp.stack(new_im))


def _config(nb, T):
    if T >= 512:
        return {"ffn_bb": 1, "ffn_tt": FF_ROWS, "e1_bb": 1, "e1_tt": 512, "gdn_bb": min(nb, 4), "mla_tq": 512,
                "s5_tt": 32}
    bb = max(1, min(nb, 512 // T))
    return {"ffn_bb": max(1, min(nb, FF_ROWS // T)), "ffn_tt": T, "e1_bb": bb, "e1_tt": T, "gdn_bb": min(nb, 4),
            "mla_tq": T, "s5_tt": min(T, 32)}


def kernel(x_prompt, x_sample, c_prompt, c_sample, cache_mla_latent, cache_mla_krope, state_gdn, state_gdn_conv, state_s5_re, state_s5_im, w_ada, b_ada, ln_g, ln_b, w_ff_in, w_ff_out, w_mix_in, gdn_conv_w, gdn_a_log, gdn_dt_bias, gdn_norm_w, mla_q_norm, mla_w_q_b, mla_kv_norm, mla_w_kv_b, w_mix_out, s5_lambda_re, s5_lambda_im, s5_log_dt, s5_b_re, s5_b_im, s5_c_re, s5_c_im, s5_d, s5_w_glu, s5_w_out):
    bp, sp, _ = x_prompt.shape
    bs, ss, _ = x_sample.shape
    assert bp == bs, "the modulation table is indexed in blocks of one stream's batch"

    mod = _ada_call(jnp.concatenate([c_prompt, c_sample], axis=0), w_ada, b_ada)
    mod3 = mod.reshape(DEPTH * (bp + bs), 1, -1)

    ffw = [[_prep_ffn(w_ff_in[l, s], w_ff_out[l, s]) for s in range(2)] for l in range(DEPTH)]
    evw = [_prep_even(w_mix_in[i], gdn_conv_w[i], gdn_a_log[i], gdn_dt_bias[i], gdn_norm_w[i], mla_q_norm[i],
                      mla_w_q_b[i], mla_kv_norm[i], mla_w_kv_b[i], w_mix_out[i]) for i in range(N_EVEN)]
    s5w = _prep_s5(s5_lambda_re, s5_lambda_im, s5_log_dt, s5_b_re, s5_b_im, s5_c_re, s5_c_im, s5_d,
                   s5_w_glu, s5_w_out)

    zeros = lambda *s: jnp.zeros(s, F32)
    outs_p = _run_trunk(
        x_prompt, 0, jnp.arange(sp, dtype=jnp.int32), mod, mod3, ffw, evw, s5w, ln_g, ln_b,
        zeros(N_EVEN, bp, GDN_H, GDN_DK, GDN_DV), zeros(N_EVEN, bp, GDN_CONV - 1, GDN_QKV),
        zeros(N_ODD, bp, S5_G, S5_P), zeros(N_ODD, bp, S5_G, S5_P), None, _config(bp, sp))

    n_past = cache_mla_latent.shape[2]
    past_kr = jnp.pad(cache_mla_krope, ((0, 0), (0, 0), (0, 0), (0, LANE - MLA_ROPE)))
    outs_s = _run_trunk(
        x_sample, bp, n_past + jnp.arange(ss, dtype=jnp.int32), mod, mod3, ffw, evw, s5w, ln_g, ln_b,
        state_gdn, state_gdn_conv, state_s5_re, state_s5_im, (cache_mla_latent, past_kr), _config(bs, ss))

    y_p, lat_p, kr_p, gdn_p, conv_p, re_p, im_p = outs_p
    y_s, lat_s, kr_s, gdn_s, conv_s, re_s, im_s = outs_s
    return (y_p, y_s, lat_p, kr_p, gdn_p, conv_p, re_p, im_p, lat_s, kr_s, gdn_s, conv_s, re_s, im_s)
```

```python
import functools
import math

import jax
import jax.numpy as jnp
from jax import lax
from jax.experimental import pallas as pl
from jax.experimental.pallas import tpu as pltpu

F32 = jnp.float32
BF16 = jnp.bfloat16

D_MODEL = 1024
DEPTH = 4
CHUNK = 64
N_EVEN = (DEPTH + 1) // 2
N_ODD = DEPTH // 2
DEEP_ALPHA = (2.0 * DEPTH) ** 0.25
EPS = 1e-6
D_FF = 2816

GDN_H = 4
GDN_DK = 128
GDN_DV = 128
GDN_CONV = 4
GDN_QKV = GDN_H * (2 * GDN_DK + GDN_DV)

MLA_H = 4
MLA_NOPE = 128
MLA_ROPE = 64
MLA_V = 128
MLA_Q_LORA = 256
MLA_KV_LORA = 128
MLA_SCALE = (MLA_NOPE + MLA_ROPE) ** -0.5
ROPE_THETA = 10000.0
LOG2_E = math.log2(math.e)
MLA_QSCALE = MLA_SCALE * LOG2_E

OFF_Z = GDN_QKV
OFF_B = OFF_Z + GDN_H * GDN_DV
OFF_A = OFF_B + GDN_H
OFF_QA = OFF_A + GDN_H
OFF_KVA = OFF_QA + MLA_Q_LORA

S5_M = 16
S5_G = D_MODEL // S5_M
S5_P = 64
S5_STATE = S5_G * S5_P
S5_KT = 256
S5_NKT = D_MODEL // S5_KT
S5_GPT = S5_KT // S5_M
S5_SPT = S5_GPT * S5_P
S5_SUB = 32

LANE = 128
SUBLANE = 8
VMEM_LIMIT = 56 * 1024 * 1024

FF_CHUNK = 256
FF_NCH = D_FF // FF_CHUNK
FF_SUB = 256
FF_ROWS = 1024
GDN_C = 128
MLA_DIAG = 256

E1_QKV = 0
E1_Z = GDN_QKV
E1_QA = E1_Z + GDN_H * GDN_DV
E1_CKV = E1_QA + MLA_Q_LORA
E1_KRA = E1_CKV + MLA_KV_LORA
E1_KRB = E1_KRA + LANE
E1_BA = E1_KRB + LANE
E1_COLS = E1_BA + LANE


def _cparams(sem):
    return pltpu.CompilerParams(dimension_semantics=sem, vmem_limit_bytes=VMEM_LIMIT)


def _dot(a, b):
    return jnp.dot(a, b, preferred_element_type=F32)


def _dot_nt(a, b):
    return lax.dot_general(a, b, (((1,), (1,)), ((), ())), preferred_element_type=F32)


def _sigmoid(x):
    return 1.0 / (1.0 + jnp.exp(-x))


def _silu(x):
    return x * _sigmoid(x)


def _layer_norm(y, g, b):
    mu = jnp.mean(y, axis=-1, keepdims=True)
    yc = y - mu
    var = jnp.mean(yc * yc, axis=-1, keepdims=True)
    return yc * lax.rsqrt(var + EPS) * g + b


def _rms_norm(y, g):
    return y * lax.rsqrt(jnp.mean(y * y, axis=-1, keepdims=True) + EPS) * g


def _split3(a):
    h = a.astype(BF16)
    r = a - h.astype(F32)
    m = r.astype(BF16)
    l = (r - m.astype(F32)).astype(BF16)
    return h, m, l


def _mm3(a, b):
    n = b.shape[1]
    ah = a.astype(BF16)
    al = (a - ah.astype(F32)).astype(BF16)
    bh = b.astype(BF16)
    bl = (b - bh.astype(F32)).astype(BF16)
    lhs = jnp.concatenate([ah, al], axis=1)
    rhs = jnp.concatenate([jnp.concatenate([bh, bl], axis=1),
                           jnp.concatenate([bh, jnp.zeros_like(bl)], axis=1)], axis=0)
    r = _dot(lhs, rhs)
    return r[:, :n] + r[:, n:]


def _ada_kernel(c_ref, w_ref, b_ref, o_ref):
    sc = _silu(c_ref[...]).astype(BF16)
    o_ref[0] = _dot(sc, w_ref[0].astype(BF16)) + b_ref[0]


def _ada_call(c_all, w_ada, b_ada):
    n = c_all.shape[0]
    ncol = w_ada.shape[-1] // D_MODEL
    return pl.pallas_call(
        _ada_kernel,
        grid=(DEPTH, ncol),
        in_specs=[
            pl.BlockSpec((n, D_MODEL), lambda l, j: (0, 0)),
            pl.BlockSpec((1, D_MODEL, D_MODEL), lambda l, j: (l, 0, j)),
            pl.BlockSpec((1, 1, D_MODEL), lambda l, j: (l, 0, j)),
        ],
        out_specs=pl.BlockSpec((1, n, D_MODEL), lambda l, j: (l, 0, j)),
        out_shape=jax.ShapeDtypeStruct((DEPTH, n, ncol * D_MODEL), F32),
        compiler_params=_cparams(("parallel", "parallel")),
        name="ada_mod",
    )(c_all, w_ada, b_ada.reshape(DEPTH, 1, -1))


def _mod_spec(mod3, layer, sub, part, off, bb):
    base = (layer * (mod3.shape[0] // DEPTH) + off) // bb
    col = sub * 3 + part
    return pl.BlockSpec((bb, 1, D_MODEL), lambda i, t: (base + i, 0, col))


def _ffn_kernel(x_ref, sh_ref, sc_ref, gt_ref, win_ref, wout_ref, lng_ref, lnb_ref, o_ref, ub_sc, h_sc,
                *, sub_rows):
    bb, tt, _ = x_ref.shape
    seg = min(tt, sub_rows)
    segs = [(b, t0) for b in range(bb) for t0 in range(0, tt, seg)]
    per_sub = sub_rows // seg

    for s in range(len(segs) // per_sub):
        rows = slice(s * sub_rows, (s + 1) * sub_rows)
        for k in range(s * per_sub, (s + 1) * per_sub):
            b, t0 = segs[k]
            ub_sc[k * seg:(k + 1) * seg] = (
                x_ref[b, t0:t0 + seg, :] * (1.0 + sc_ref[b]) + sh_ref[b]).astype(BF16)
        for c in range(FF_NCH):
            cols = slice(c * FF_CHUNK, (c + 1) * FF_CHUNK)
            gate = _dot(ub_sc[rows], win_ref[:, c * FF_CHUNK:(c + 1) * FF_CHUNK])
            up = _dot(ub_sc[rows], win_ref[:, D_FF + c * FF_CHUNK:D_FF + (c + 1) * FF_CHUNK])
            h_sc[rows, cols] = (_silu(gate) * up).astype(BF16)
        h = _dot(h_sc[rows], wout_ref[...])
        for j in range(per_sub):
            b, t0 = segs[s * per_sub + j]
            y = DEEP_ALPHA * x_ref[b, t0:t0 + seg, :] + 0.5 * (1.0 + gt_ref[b]) * h[j * seg:(j + 1) * seg]
            o_ref[b, t0:t0 + seg, :] = _layer_norm(y, lng_ref[...], lnb_ref[...])


def _ffn_call(x, mod3, layer, sub, off, win_r, wout_r, lng, lnb, bb, tt):
    nb, T, _ = x.shape
    const2 = lambda i, t: (0, 0)
    rows = bb * tt
    sub_rows = min(FF_SUB, rows)
    assert rows % sub_rows == 0 and (tt % sub_rows == 0 or sub_rows % tt == 0)
    tok = pl.BlockSpec((bb, tt, D_MODEL), lambda i, t: (i, t, 0))
    return pl.pallas_call(
        functools.partial(_ffn_kernel, sub_rows=sub_rows),
        grid=(nb // bb, T // tt),
        in_specs=[
            tok,
            _mod_spec(mod3, layer, sub, 0, off, bb),
            _mod_spec(mod3, layer, sub, 1, off, bb),
            _mod_spec(mod3, layer, sub, 2, off, bb),
            pl.BlockSpec(win_r.shape, const2, pipeline_mode=pl.Buffered(1)),
            pl.BlockSpec(wout_r.shape, const2, pipeline_mode=pl.Buffered(1)),
            pl.BlockSpec((1, D_MODEL), const2),
            pl.BlockSpec((1, D_MODEL), const2),
        ],
        out_specs=tok,
        out_shape=jax.ShapeDtypeStruct(x.shape, F32),
        scratch_shapes=[pltpu.VMEM((rows, D_MODEL), BF16), pltpu.VMEM((rows, D_FF), BF16)],
        compiler_params=_cparams(("parallel", "parallel")),
        name="ffn",
    )(x, mod3, mod3, mod3, win_r, wout_r, lng, lnb)


def _e1_kernel(x_ref, sh_ref, sc_ref, w_ref, conv0_ref, convw_ref, cos_ref, sin_ref, cost_ref, sint_ref,
               qn_ref, kvn_ref, wqa_ref, wqb_ref, wkn_ref, wv_ref,
               qkvn_ref, z_ref, bg_ref, qa_ref, ka_ref, vv_ref, ckv_ref, kr_ref, tail_ref, tail_sc,
               *, feature_major):
    bb, tt, _ = x_ref.shape
    rows = bb * tt
    assert bb == 1 or not feature_major

    @pl.when(pl.program_id(1) == 0)
    def _():
        tail_sc[...] = conv0_ref[...]

    u = x_ref[...] * (1.0 + sc_ref[...]) + sh_ref[...]
    ub = u.reshape(rows, D_MODEL).astype(BF16)

    pre = _dot(ub, w_ref[:, E1_QKV:E1_Z])
    for b in range(bb):
        pre_b = pre[b * tt:(b + 1) * tt]
        hist = jnp.concatenate([tail_sc[b], pre_b], axis=0)
        y = pre_b * convw_ref[GDN_CONV - 1:GDN_CONV, :]
        for lag in range(1, GDN_CONV):
            w_row = convw_ref[GDN_CONV - 1 - lag:GDN_CONV - lag, :]
            y = y + pltpu.roll(hist, lag, 0)[SUBLANE:] * w_row
        tail_sc[b] = pre_b[tt - SUBLANE:]
        y = _silu(y)
        for h in range(GDN_H):
            qh = y[:, h * GDN_DK:(h + 1) * GDN_DK]
            qkvn_ref[b, :, h * GDN_DK:(h + 1) * GDN_DK] = (
                qh * lax.rsqrt(jnp.sum(qh * qh, axis=-1, keepdims=True) + EPS) * (GDN_DK ** -0.5))
            ko = GDN_H * GDN_DK + h * GDN_DK
            kh = y[:, ko:ko + GDN_DK]
            qkvn_ref[b, :, ko:ko + GDN_DK] = kh * lax.rsqrt(jnp.sum(kh * kh, axis=-1, keepdims=True) + EPS)
        vo = 2 * GDN_H * GDN_DK
        qkvn_ref[b, :, vo:] = y[:, vo:]
    tail_ref[...] = tail_sc[...]

    z_ref[...] = _dot(ub, w_ref[:, E1_Z:E1_QA]).reshape(bb, tt, -1)
    bg_ref[...] = _dot(ub, w_ref[:, E1_BA:E1_COLS]).reshape(bb, tt, -1)

    cos = cos_ref[...]
    sin = sin_ref[...]
    if bb > 1:
        cos = jnp.concatenate([cos] * bb, axis=0)
        sin = jnp.concatenate([sin] * bb, axis=0)

    cq = _rms_norm(_dot(ub, w_ref[:, E1_QA:E1_CKV]), qn_ref[...])
    if feature_major:
        cq_t = cq.T.astype(BF16)
        qa = _dot(wqa_ref[...], cq_t)
        qb = _dot(wqb_ref[...], cq_t)
        for h in range(MLA_H):
            o = 2 * LANE * h
            qa_ref[0, o:o + LANE, :] = (qa[o:o + LANE] * MLA_QSCALE).astype(BF16)
            rot = qa[o + LANE:o + 2 * LANE] * cost_ref[...] + qb[h * LANE:(h + 1) * LANE] * sint_ref[...]
            qa_ref[0, o + LANE:o + 2 * LANE, :] = (rot * MLA_QSCALE).astype(BF16)
    else:
        cqb = cq.astype(BF16)
        qa = _dot(cqb, wqa_ref[...])
        qb = _dot(cqb, wqb_ref[...])
        for h in range(MLA_H):
            o = 2 * LANE * h
            qa_ref[:, :, o:o + LANE] = (qa[:, o:o + LANE] * MLA_QSCALE).astype(BF16).reshape(bb, tt, LANE)
            rot = qa[:, o + LANE:o + 2 * LANE] * cos + qb[:, h * LANE:(h + 1) * LANE] * sin
            qa_ref[:, :, o + LANE:o + 2 * LANE] = (rot * MLA_QSCALE).astype(BF16).reshape(bb, tt, LANE)

    ckv = _rms_norm(_dot(ub, w_ref[:, E1_CKV:E1_KRA]), kvn_ref[...])
    ckv_ref[...] = ckv.reshape(bb, tt, -1)
    kr = _dot(ub, w_ref[:, E1_KRA:E1_KRB]) * cos + _dot(ub, w_ref[:, E1_KRB:E1_BA]) * sin
    kr_ref[...] = kr[:, :MLA_ROPE].reshape(bb, tt, -1)
    cb = ckv.astype(BF16)
    krb = kr.astype(BF16).reshape(bb, tt, LANE)
    kn = _dot(cb, wkn_ref[...])
    for h in range(MLA_H):
        o = 2 * LANE * h
        ka_ref[:, :, o:o + LANE] = kn[:, h * LANE:(h + 1) * LANE].astype(BF16).reshape(bb, tt, LANE)
        ka_ref[:, :, o + LANE:o + 2 * LANE] = krb
    if feature_major:
        vv_ref[0] = _dot(wv_ref[...], ckv.T.astype(BF16)).astype(BF16)
    else:
        vv_ref[...] = _dot(cb, wv_ref[...]).astype(BF16).reshape(bb, tt, -1)


def _e1_call(x, mod3, layer, off, ew, conv0p, rope, bb, tt, feature_major):
    nb, T, _ = x.shape
    tok = lambda i, t: (i, t, 0)
    const2 = lambda i, t: (0, 0)

    def full(a):
        return pl.BlockSpec(a.shape, const2)

    out_dims = [(GDN_QKV, F32), (GDN_H * GDN_DV, F32), (LANE, F32), (MLA_H * 2 * LANE, BF16),
                (MLA_H * 2 * LANE, BF16), (MLA_H * MLA_V, BF16), (MLA_KV_LORA, F32), (MLA_ROPE, F32)]
    out_shape = [jax.ShapeDtypeStruct((nb, T, d), dt) for d, dt in out_dims]
    out_specs = [pl.BlockSpec((bb, tt, d), tok) for d, _ in out_dims]
    wqa, wqb, wv = ew["wqa"], ew["wqb"], ew["wv"]
    if feature_major:
        wqa, wqb, wv = ew["wqa_t"], ew["wqb_t"], ew["wv_t"]
        for idx in (3, 5):
            d = out_dims[idx][0]
            out_shape[idx] = jax.ShapeDtypeStruct((nb, d, T), BF16)
            out_specs[idx] = pl.BlockSpec((bb, d, tt), lambda i, t: (i, 0, t))
    out_shape.append(jax.ShapeDtypeStruct((nb, SUBLANE, GDN_QKV), F32))
    out_specs.append(pl.BlockSpec((bb, SUBLANE, GDN_QKV), lambda i, t: (i, 0, 0)))
    cos_t, sin_t = rope["rows"]
    cos_f, sin_f = rope["cols"]
    return pl.pallas_call(
        functools.partial(_e1_kernel, feature_major=feature_major),
        grid=(nb // bb, T // tt),
        in_specs=[
            pl.BlockSpec((bb, tt, D_MODEL), tok),
            _mod_spec(mod3, layer, 1, 0, off, bb),
            _mod_spec(mod3, layer, 1, 1, off, bb),
            full(ew["w_e1"]),
            pl.BlockSpec((bb, SUBLANE, GDN_QKV), lambda i, t: (i, 0, 0)),
            full(ew["conv_w"]),
            pl.BlockSpec((tt, LANE), lambda i, t: (t, 0)),
            pl.BlockSpec((tt, LANE), lambda i, t: (t, 0)),
            pl.BlockSpec((LANE, tt), lambda i, t: (0, t)),
            pl.BlockSpec((LANE, tt), lambda i, t: (0, t)),
            full(ew["q_norm"]), full(ew["kv_norm"]),
            full(wqa), full(wqb), full(ew["wkn"]), full(wv),
        ],
        out_specs=out_specs,
        out_shape=out_shape,
        scratch_shapes=[pltpu.VMEM((bb, SUBLANE, GDN_QKV), F32)],
        compiler_params=_cparams(("parallel", "arbitrary")),
        name="mix_proj",
    )(x, mod3, mod3, ew["w_e1"], conv0p, ew["conv_w"], cos_t, sin_t, cos_f, sin_f, ew["q_norm"], ew["kv_norm"],
      wqa, wqb, ew["wkn"], wv)


def _gdn_kernel(qkvn_ref, z_ref, bg_ref, s0_ref, alog_ref, dtb_ref, nw_ref, o_ref, s_ref, s_sc, *, n_valid):
    bb = qkvn_ref.shape[0]
    C = GDN_C
    units = [(b, h) for b in range(bb) for h in range(GDN_H)]

    @pl.when(pl.program_id(1) == 0)
    def _():
        s_sc[...] = s0_ref[...]

    row = lax.broadcasted_iota(jnp.int32, (C, C), 0)
    col = lax.broadcasted_iota(jnp.int32, (C, C), 1)
    incl = row >= col
    strict = row > col
    tri = jnp.where(incl, 1.0, 0.0).astype(BF16)
    eye = jnp.where(row == col, 1.0, 0.0)

    beta_all, G_all, G_t = [], [], []
    for b in range(bb):
        bg = bg_ref[b]
        beta_all.append(_sigmoid(bg))
        xs = bg + dtb_ref[...]
        softplus = jnp.maximum(xs, 0.0) + jnp.log(1.0 + jnp.exp(-jnp.abs(xs)))
        g_all = -jnp.exp(alog_ref[...]) * softplus
        if n_valid < C:
            g_all = jnp.where(row < n_valid, g_all, 0.0)
        g3 = _dot(tri, jnp.concatenate(_split3(g_all), axis=1))
        G_all.append(g3[:, :LANE] + g3[:, LANE:2 * LANE] + g3[:, 2 * LANE:])
        G_t.append(G_all[b].T)

    def load(b, h, part):
        o = (part * GDN_H + h) * GDN_DK
        return qkvn_ref[b, :, o:o + GDN_DK]

    q = [load(b, h, 0) for b, h in units]
    k = [load(b, h, 1) for b, h in units]
    v = [load(b, h, 2) for b, h in units]
    kb = [x.astype(BF16) for x in k]
    beta = [beta_all[b][:, h:h + 1] for b, h in units]
    Gc = [G_all[b][:, GDN_H + h:GDN_H + h + 1] for b, h in units]
    Gr = [G_t[b][GDN_H + h:GDN_H + h + 1, :] for b, h in units]
    decay = [jnp.where(incl, jnp.exp(jnp.where(incl, c - r, 0.0)), 0.0) for c, r in zip(Gc, Gr)]
    kq = [_dot_nt(jnp.concatenate([y, x.astype(BF16)], axis=0), y) for x, y in zip(q, kb)]
    lower = [jnp.where(strict, bt * d * m[:C], 0.0) for bt, d, m in zip(beta, decay, kq)]
    inv = [eye - m for m in lower]
    pw = [_mm3(m, m) for m in lower]
    n_sq = int(math.log2(C)) - 1
    for it in range(n_sq):
        inv = [a + _mm3(a, p) for a, p in zip(inv, pw)]
        if it + 1 < n_sq:
            pw = [_mm3(p, p) for p in pw]
    gam = [jnp.exp(c) for c in Gc]
    sol_v = [_mm3(a, bt * x) for a, bt, x in zip(inv, beta, v)]
    sol_k = [_mm3(a, (bt * g) * x).astype(BF16) for a, bt, g, x in zip(inv, beta, gam, k)]
    attn = [(m[C:] * d).astype(BF16) for m, d in zip(kq, decay)]
    qg = [(g * x).astype(BF16) for g, x in zip(gam, q)]
    g_end = [c[C - 1:C, :] for c in Gc]
    kd = [(jnp.exp(e - c) * x).T.astype(BF16) for e, c, x in zip(g_end, Gc, k)]

    s0 = [s_sc[b, h] for b, h in units]
    with_s0 = [_dot(jnp.concatenate([w, x], axis=0), s.astype(BF16)) for w, x, s in zip(sol_k, qg, s0)]
    ub = [(a - r[:C]).astype(BF16) for a, r in zip(sol_v, with_s0)]
    with_u = [_dot(jnp.concatenate([a, x], axis=0), y) for a, x, y in zip(attn, kd, ub)]
    o = [r[C:] + t[:C] for r, t in zip(with_s0, with_u)]
    s1 = [jnp.exp(e) * s + t[C:] for e, s, t in zip(g_end, s0, with_u)]
    for i, (b, h) in enumerate(units):
        s_sc[b, h] = s1[i]
        zh = z_ref[b, :, h * GDN_DV:(h + 1) * GDN_DV]
        o_ref[b, :, h * GDN_DV:(h + 1) * GDN_DV] = (_rms_norm(o[i], nw_ref[...]) * _silu(zh)).astype(BF16)
    s_ref[...] = s_sc[...]


def _gdn_call(qkvn, z, bg, s0, ew, bb, n_valid):
    nb, T, _ = qkvn.shape
    tok = lambda i, c: (i, c, 0)
    const2 = lambda i, c: (0, 0)
    st = lambda i, c: (i, 0, 0, 0)
    return pl.pallas_call(
        functools.partial(_gdn_kernel, n_valid=n_valid),
        grid=(nb // bb, T // GDN_C),
        in_specs=[
            pl.BlockSpec((bb, GDN_C, GDN_QKV), tok),
            pl.BlockSpec((bb, GDN_C, GDN_H * GDN_DV), tok),
            pl.BlockSpec((bb, GDN_C, LANE), tok),
            pl.BlockSpec((bb, GDN_H, GDN_DK, GDN_DV), st),
            pl.BlockSpec((1, LANE), const2),
            pl.BlockSpec((1, LANE), const2),
            pl.BlockSpec((1, GDN_DV), const2),
        ],
        out_specs=[pl.BlockSpec((bb, GDN_C, GDN_H * GDN_DV), tok),
                   pl.BlockSpec((bb, GDN_H, GDN_DK, GDN_DV), st)],
        out_shape=[jax.ShapeDtypeStruct((nb, T, GDN_H * GDN_DV), BF16),
                   jax.ShapeDtypeStruct((nb, GDN_H, GDN_DK, GDN_DV), F32)],
        scratch_shapes=[pltpu.VMEM((bb, GDN_H, GDN_DK, GDN_DV), F32)],
        compiler_params=_cparams(("parallel", "arbitrary")),
        name="gdn",
    )(qkvn, z, bg, s0, ew["alog"], ew["dtb"], ew["norm_w"])


def _attn_init(m_sc, l_sc, acc_sc):
    m_sc[...] = jnp.full_like(m_sc, -jnp.inf)
    l_sc[...] = jnp.zeros_like(l_sc)
    acc_sc[...] = jnp.zeros_like(acc_sc)


def _mix_out(og, heads_out, x_ref, gt_ref, wout_ref, lng_ref, lnb_ref, o_ref):
    mix = _dot(jnp.concatenate([og] + heads_out, axis=-1), wout_ref[...])
    y = DEEP_ALPHA * x_ref[0] + (1.0 + gt_ref[0]) * mix
    o_ref[0] = _layer_norm(y, lng_ref[...], lnb_ref[...])


def _mla_prompt_kernel(qa_ref, ka_ref, vv_ref, og_ref, x_ref, gt_ref, wout_ref, lng_ref, lnb_ref, o_ref,
                       m_sc, l_sc, acc_sc):
    tq = qa_ref.shape[2]
    i = pl.program_id(1)
    _attn_init(m_sc, l_sc, acc_sc)
    heads = range(MLA_H)
    def block(ks, nk, q0, nq, q_off):
        qs = slice(q0, q0 + nq)
        s = [_dot(ka_ref[0, pl.ds(ks, nk), 2 * LANE * h:2 * LANE * (h + 1)],
                  qa_ref[0, 2 * LANE * h:2 * LANE * (h + 1), qs]) for h in heads]
        if q_off is not None:
            vis = ((q_off + lax.broadcasted_iota(jnp.int32, (nk, nq), 1)) // CHUNK
                   >= lax.broadcasted_iota(jnp.int32, (nk, nq), 0) // CHUNK)
            s = [jnp.where(vis, x, -jnp.inf) for x in s]
        m_prev = [m_sc[h, :, qs] for h in heads]
        m_new = [jnp.maximum(m, jnp.max(x, axis=0, keepdims=True)) for m, x in zip(m_prev, s)]
        p = [jnp.exp2(x - m) for x, m in zip(s, m_new)]
        alpha = [jnp.exp2(a - b) for a, b in zip(m_prev, m_new)]
        pv = [_dot(vv_ref[0, MLA_V * h:MLA_V * (h + 1), pl.ds(ks, nk)], p[h].astype(BF16)) for h in heads]
        for h in heads:
            l_sc[h, :, qs] = alpha[h] * l_sc[h, :, qs] + jnp.sum(p[h], axis=0, keepdims=True)
            acc_sc[h, :, qs] = alpha[h] * acc_sc[h, :, qs] + pv[h]
            m_sc[h, :, qs] = m_new[h]

    def body(j, carry):
        block(pl.multiple_of(j * tq, tq), tq, 0, tq, None)
        return carry

    lax.fori_loop(0, i, body, 0)
    sub = min(tq, MLA_DIAG)
    d0 = pl.multiple_of(i * tq, tq)
    for kb in range(tq // sub):
        ks = pl.multiple_of(d0 + kb * sub, sub)
        block(ks, sub, kb * sub, sub, 0)
        if (kb + 1) * sub < tq:
            block(ks, sub, (kb + 1) * sub, tq - (kb + 1) * sub, None)
    heads_out = [(acc_sc[h] / l_sc[h]).T.astype(BF16) for h in heads]
    _mix_out(og_ref[0], heads_out, x_ref, gt_ref, wout_ref, lng_ref, lnb_ref, o_ref)


def _mla_sample_kernel(qa_ref, pckv_ref, pkr_ref, ka_ref, vv_ref, og_ref, x_ref, gt_ref, wknt_ref, wv_ref,
                       wout_ref, lng_ref, lnb_ref, o_ref):
    ckv = pckv_ref[0]
    kn_t = _dot(wknt_ref[...], ckv.T.astype(BF16)).astype(BF16)
    kr_t = pkr_ref[0].T.astype(BF16)
    vp = _dot(ckv.astype(BF16), wv_ref[...]).astype(BF16)
    heads_out = []
    for h in range(MLA_H):
        q = qa_ref[0, :, 2 * LANE * h:2 * LANE * (h + 1)]
        k_t = jnp.concatenate([kn_t[h * MLA_NOPE:(h + 1) * MLA_NOPE], kr_t], axis=0)
        s_p = _dot(q, k_t)
        s_n = _dot_nt(q, ka_ref[0, :, 2 * LANE * h:2 * LANE * (h + 1)])
        m = jnp.maximum(jnp.max(s_p, axis=-1, keepdims=True), jnp.max(s_n, axis=-1, keepdims=True))
        p_p = jnp.exp2(s_p - m)
        p_n = jnp.exp2(s_n - m)
        l = jnp.sum(p_p, axis=-1, keepdims=True) + jnp.sum(p_n, axis=-1, keepdims=True)
        o = (_dot(p_p.astype(BF16), vp[:, MLA_V * h:MLA_V * (h + 1)])
             + _dot(p_n.astype(BF16), vv_ref[0, :, MLA_V * h:MLA_V * (h + 1)]))
        heads_out.append((o / l).astype(BF16))
    _mix_out(og_ref[0], heads_out, x_ref, gt_ref, wout_ref, lng_ref, lnb_ref, o_ref)


def _mla_prompt_call(qa, ka, vv, og, x, mod3, layer, off, w_out, lng, lnb, tq):
    nb, T, _ = x.shape
    tok = lambda b, i: (b, i, 0)
    seq = lambda b, i: (b, 0, 0)
    const2 = lambda b, i: (0, 0)
    return pl.pallas_call(
        _mla_prompt_kernel,
        grid=(nb, T // tq),
        in_specs=[
            pl.BlockSpec((1, qa.shape[1], tq), lambda b, i: (b, 0, i)),
            pl.BlockSpec((1,) + ka.shape[1:], seq),
            pl.BlockSpec((1,) + vv.shape[1:], seq),
            pl.BlockSpec((1, tq, og.shape[-1]), tok),
            pl.BlockSpec((1, tq, D_MODEL), tok),
            _mod_spec(mod3, layer, 1, 2, off, 1),
            pl.BlockSpec(w_out.shape, const2),
            pl.BlockSpec((1, D_MODEL), const2),
            pl.BlockSpec((1, D_MODEL), const2),
        ],
        out_specs=pl.BlockSpec((1, tq, D_MODEL), tok),
        out_shape=jax.ShapeDtypeStruct(x.shape, F32),
        scratch_shapes=[pltpu.VMEM((MLA_H, 1, tq), F32), pltpu.VMEM((MLA_H, 1, tq), F32),
                        pltpu.VMEM((MLA_H, MLA_V, tq), F32)],
        compiler_params=_cparams(("parallel", "arbitrary")),
        name="mla_prompt",
    )(qa, ka, vv, og, x, mod3, w_out, lng, lnb)


def _mla_sample_call(qa, past_ckv, past_kr128, ka, vv, og, x, mod3, layer, off, ew, lng, lnb):
    nb, T, _ = x.shape
    seq = lambda b, i: (b, 0, 0)
    const2 = lambda b, i: (0, 0)

    def whole(a):
        return pl.BlockSpec((1,) + a.shape[1:], seq)

    def full(a):
        return pl.BlockSpec(a.shape, const2)

    return pl.pallas_call(
        _mla_sample_kernel,
        grid=(nb, 1),
        in_specs=[whole(qa), whole(past_ckv), whole(past_kr128), whole(ka), whole(vv), whole(og), whole(x),
                  _mod_spec(mod3, layer, 1, 2, off, 1),
                  full(ew["wkn_t"]), full(ew["wv"]), full(ew["w_out"]), full(lng), full(lnb)],
        out_specs=whole(x),
        out_shape=jax.ShapeDtypeStruct(x.shape, F32),
        compiler_params=_cparams(("parallel", "arbitrary")),
        name="mla_sample",
    )(qa, past_ckv, past_kr128, ka, vv, og, x, mod3, ew["wkn_t"], ew["wv"], ew["w_out"], lng, lnb)


def _s5prep_kernel(lre_ref, lim_ref, ldt_ref, br_ref, bi_ref, abre_ref, abim_ref, bbre_ref, bbim_ref):
    dt = jnp.exp(ldt_ref[0])
    lr = jnp.minimum(lre_ref[0], -1e-4)
    li = lim_ref[0]
    mag = jnp.exp(lr * dt)
    ab_re = mag * jnp.cos(li * dt)
    ab_im = mag * jnp.sin(li * dt)
    den = lr * lr + li * li
    coef_re = ((ab_re - 1.0) * lr + ab_im * li) / den
    coef_im = (ab_im * lr - (ab_re - 1.0) * li) / den
    br = br_ref[0]
    bi = bi_ref[0]
    abre_ref[0] = ab_re
    abim_ref[0] = ab_im
    bbre_ref[0] = coef_re * br - coef_im * bi
    bbim_ref[0] = coef_re * bi + coef_im * br


def _s5prep_call(lre_x, lim_x, ldt_x, br_t, bi_t):
    n = lre_x.shape[0]
    blk = pl.BlockSpec((1, D_MODEL, S5_P), lambda i: (i, 0, 0))
    return pl.pallas_call(
        _s5prep_kernel,
        grid=(n,),
        in_specs=[blk, blk, pl.BlockSpec((1, D_MODEL, 1), lambda i: (i, 0, 0)), blk, blk],
        out_specs=[blk] * 4,
        out_shape=[jax.ShapeDtypeStruct((n, D_MODEL, S5_P), F32)] * 4,
        compiler_params=_cparams(("parallel",)),
        name="s5_discretise",
    )(lre_x, lim_x, ldt_x, br_t, bi_t)


def _s5_kernel(x_ref, sh_ref, sc_ref, gt_ref, h0re_ref, h0im_ref, abre_ref, abim_ref, bb_ref, cre_ref, cim_ref,
               dskip_ref, wglu_ref, wout_ref, lng_ref, lnb_ref, o_ref, hre_ref, him_ref,
               hre_sc, him_sc, xt_sc, ub_sc, st_sc, xb_sc, y_sc):
    nb, tt, _ = x_ref.shape
    n_sub, ts = xt_sc.shape[0], xt_sc.shape[1]
    rows = ts * nb
    half = S5_SPT // 2

    @pl.when(pl.program_id(0) == 0)
    def _():
        hre_sc[...] = h0re_ref[...]
        him_sc[...] = h0im_ref[...]

    for s in range(n_sub):
        frames = slice(s * ts, (s + 1) * ts)
        xt_sc[s] = pltpu.einshape("btd->tbd", x_ref[:, frames, :])

        def modulated():
            return (xt_sc[s] * (1.0 + sc_ref[...]) + sh_ref[...]).reshape(rows, D_MODEL)

        ub_sc[s] = modulated().astype(BF16)
        for kt in range(S5_NKT):
            st_sc[s, kt] = _dot(ub_sc[s, :, kt * S5_KT:(kt + 1) * S5_KT], bb_ref[kt])
        for kt in range(S5_NKT):
            for lc in range(2):
                so = kt * S5_SPT + lc * half
                re_cols = slice(lc * half, (lc + 1) * half)
                im_cols = slice(S5_SPT + lc * half, S5_SPT + (lc + 1) * half)
                a_re = jnp.broadcast_to(abre_ref[:, so:so + half], (nb, half))
                a_im = jnp.broadcast_to(abim_ref[:, so:so + half], (nb, half))
                h_re = hre_sc[:, so:so + half]
                h_im = him_sc[:, so:so + half]
                for t in range(ts):
                    frame = slice(t * nb, (t + 1) * nb)
                    n_re = a_re * h_re - a_im * h_im + st_sc[s, kt, frame, re_cols]
                    n_im = a_re * h_im + a_im * h_re + st_sc[s, kt, frame, im_cols]
                    xb_sc[s, kt, frame, re_cols] = n_re.astype(BF16)
                    xb_sc[s, kt, frame, im_cols] = n_im.astype(BF16)
                    h_re, h_im = n_re, n_im
                hre_sc[:, so:so + half] = h_re
                him_sc[:, so:so + half] = h_im
            y_sc[s, :, kt * S5_KT:(kt + 1) * S5_KT] = (
                _dot(xb_sc[s, kt, :, :S5_SPT], cre_ref[kt]) - _dot(xb_sc[s, kt, :, S5_SPT:], cim_ref[kt]))

        y = y_sc[s] + dskip_ref[...] * modulated()
        yg = 0.5 * y * (1.0 + jnp.tanh(math.sqrt(2.0 / math.pi) * (y + 0.044715 * (y * y * y))))
        glu = _dot(yg.astype(BF16), wglu_ref[...])
        mix = _dot((yg * _sigmoid(glu)).astype(BF16), wout_ref[...])
        out = DEEP_ALPHA * xt_sc[s] + (1.0 + gt_ref[...]) * mix.reshape(ts, nb, D_MODEL)
        o_ref[:, frames, :] = pltpu.einshape("tbd->btd", _layer_norm(out, lng_ref[...], lnb_ref[...]))

    @pl.when(pl.program_id(0) == pl.num_programs(0) - 1)
    def _():
        hre_ref[...] = hre_sc[...]
        him_ref[...] = him_sc[...]


def _s5_call(x, mod, layer, stream, h0re, h0im, sw, lng, lnb, tt):
    nb, T, _ = x.shape
    ts = min(tt, S5_SUB)
    n_sub = tt // ts
    rows = ts * nb
    const2 = lambda t: (0, 0)
    const3 = lambda t: (0, 0, 0)

    def mspec(part):
        return pl.BlockSpec((1, nb, D_MODEL), lambda t: (layer, stream, 3 + part))

    def full(a):
        return pl.BlockSpec(a.shape, const2 if a.ndim == 2 else const3, pipeline_mode=pl.Buffered(1))

    return pl.pallas_call(
        _s5_kernel,
        grid=(T // tt,),
        in_specs=[
            pl.BlockSpec((nb, tt, D_MODEL), lambda t: (0, t, 0)),
            mspec(0), mspec(1), mspec(2),
            full(h0re), full(h0im), full(sw["ab_re"]), full(sw["ab_im"]),
            full(sw["bb"]), full(sw["c_re"]), full(sw["c_im"]), full(sw["d_skip"]),
            full(sw["w_glu"]), full(sw["w_out"]), full(lng), full(lnb),
        ],
        out_specs=[pl.BlockSpec((nb, tt, D_MODEL), lambda t: (0, t, 0)),
                   pl.BlockSpec((nb, S5_STATE), const2), pl.BlockSpec((nb, S5_STATE), const2)],
        out_shape=[jax.ShapeDtypeStruct(x.shape, F32),
                   jax.ShapeDtypeStruct((nb, S5_STATE), F32), jax.ShapeDtypeStruct((nb, S5_STATE), F32)],
        scratch_shapes=[pltpu.VMEM((nb, S5_STATE), F32), pltpu.VMEM((nb, S5_STATE), F32),
                        pltpu.VMEM((n_sub, ts, nb, D_MODEL), F32), pltpu.VMEM((n_sub, rows, D_MODEL), BF16),
                        pltpu.VMEM((n_sub, S5_NKT, rows, 2 * S5_SPT), F32),
                        pltpu.VMEM((n_sub, S5_NKT, rows, 2 * S5_SPT), BF16),
                        pltpu.VMEM((n_sub, rows, D_MODEL), F32)],
        compiler_params=_cparams(("arbitrary",)),
        name="s5_mixer",
    )(x, mod, mod, mod, h0re, h0im, sw["ab_re"], sw["ab_im"], sw["bb"], sw["c_re"], sw["c_im"],
      sw["d_skip"], sw["w_glu"], sw["w_out"], lng, lnb)


def _prep_ffn(w_in, w_out):
    return w_in.astype(BF16), w_out.astype(BF16)


def _prep_even(w_in, conv_w, a_log, dt_bias, norm_w, q_norm, w_q_b, kv_norm, w_kv_b, w_out):
    half = MLA_ROPE // 2
    zcol = lambda n: jnp.zeros((w_in.shape[0], n), w_in.dtype)
    k1 = w_in[:, OFF_KVA + MLA_KV_LORA:OFF_KVA + MLA_KV_LORA + half]
    k2 = w_in[:, OFF_KVA + MLA_KV_LORA + half:OFF_KVA + MLA_KV_LORA + MLA_ROPE]
    w_e1 = jnp.concatenate([
        w_in[:, :OFF_B],
        w_in[:, OFF_QA:OFF_KVA + MLA_KV_LORA],
        k1, k2, zcol(LANE - MLA_ROPE),
        -k2, k1, zcol(LANE - MLA_ROPE),
        w_in[:, OFF_B:OFF_QA], zcol(LANE - 2 * GDN_H),
    ], axis=1).astype(BF16)
    zq = jnp.zeros((MLA_Q_LORA, LANE - MLA_ROPE), w_q_b.dtype)
    qa_cols, qb_cols = [], []
    for h in range(MLA_H):
        o = h * (MLA_NOPE + MLA_ROPE)
        r1 = w_q_b[:, o + MLA_NOPE:o + MLA_NOPE + half]
        r2 = w_q_b[:, o + MLA_NOPE + half:o + MLA_NOPE + MLA_ROPE]
        qa_cols += [w_q_b[:, o:o + MLA_NOPE], r1, r2, zq]
        qb_cols += [-r2, r1, zq]
    kv = w_kv_b.reshape(MLA_KV_LORA, MLA_H, MLA_NOPE + MLA_V)
    pad = lambda v: jnp.pad(v.astype(F32), (GDN_H, LANE - 2 * GDN_H)).reshape(1, LANE)
    return {
        "w_e1": w_e1,
        "conv_w": conv_w.astype(F32),
        "alog": pad(a_log), "dtb": pad(dt_bias),
        "norm_w": norm_w.reshape(1, -1).astype(F32),
        "q_norm": q_norm.reshape(1, -1).astype(F32),
        "kv_norm": kv_norm.reshape(1, -1).astype(F32),
        "wqa": jnp.concatenate(qa_cols, axis=1).astype(BF16),
        "wqb": jnp.concatenate(qb_cols, axis=1).astype(BF16),
        "wqa_t": jnp.concatenate(qa_cols, axis=1).T.astype(BF16),
        "wqb_t": jnp.concatenate(qb_cols, axis=1).T.astype(BF16),
        "wv_t": kv[:, :, MLA_NOPE:].reshape(MLA_KV_LORA, -1).T.astype(BF16),
        "wkn": kv[:, :, :MLA_NOPE].reshape(MLA_KV_LORA, -1).astype(BF16),
        "wkn_t": kv[:, :, :MLA_NOPE].reshape(MLA_KV_LORA, -1).T.astype(BF16),
        "wv": kv[:, :, MLA_NOPE:].reshape(MLA_KV_LORA, -1).astype(BF16),
        "w_out": w_out.astype(BF16),
    }


def _rope_tables(pos):
    half = MLA_ROPE // 2
    inv = ROPE_THETA ** (-jnp.arange(half, dtype=F32) / half)
    ang = pos.astype(F32)[:, None] * inv[None, :]
    z = jnp.zeros((pos.shape[0], LANE - MLA_ROPE), F32)
    cos, sin = jnp.cos(ang), jnp.sin(ang)
    cos, sin = jnp.concatenate([cos, cos, z], axis=1), jnp.concatenate([sin, sin, z], axis=1)
    return {"rows": (cos, sin), "cols": (cos.T, sin.T)}


def _prep_s5(lam_re, lam_im, log_dt, b_re, b_im, c_re, c_im, d_skip, w_glu, w_out):
    n = lam_re.shape[0]
    rep = lambda a: jnp.repeat(a.astype(F32), S5_M, axis=1)
    to_gmp = lambda a: jnp.transpose(a.astype(F32), (0, 1, 3, 2)).reshape(n, D_MODEL, S5_P)
    ab_re, ab_im, bb_re, bb_im = _s5prep_call(
        rep(lam_re), rep(lam_im), jnp.repeat(log_dt.astype(F32), S5_M, axis=1)[..., None],
        to_gmp(b_re), to_gmp(b_im))
    eye = jnp.eye(S5_GPT, dtype=F32)

    def embed_in(bb):
        t = bb.reshape(n, S5_NKT, S5_GPT, S5_M, S5_P)
        return jnp.einsum("nkgmp,gh->nkgmhp", t, eye).reshape(n, S5_NKT, S5_KT, S5_SPT)

    def embed_out(c):
        t = c.astype(F32).reshape(n, S5_NKT, S5_GPT, S5_M, S5_P)
        return jnp.einsum("nkgmp,gh->nkhpgm", t, eye).reshape(n, S5_NKT, S5_SPT, S5_KT)

    pick = lambda a: a.reshape(n, S5_G, S5_M, S5_P)[:, :, 0, :].reshape(n, 1, S5_STATE)
    return {
        "ab_re": pick(ab_re), "ab_im": pick(ab_im),
        "bb": jnp.concatenate([embed_in(bb_re), embed_in(bb_im)], axis=-1).astype(BF16),
        "c_re": embed_out(c_re).astype(BF16), "c_im": embed_out(c_im).astype(BF16),
        "d_skip": d_skip.reshape(n, 1, D_MODEL).astype(F32),
        "w_glu": w_glu.astype(BF16), "w_out": w_out.astype(BF16),
    }


def _pad_rows(a, n):
    return jnp.pad(a, ((0, 0), (0, n - a.shape[1]), (0, 0)))


def _run_trunk(x, off, pos, mod, mod3, ffw, evw, s5w, ln_g, ln_b, gdn0, conv0, s5re0, s5im0, past, cfg):
    nb, T, _ = x.shape
    rope = _rope_tables(pos)
    new_ckv, new_kr, new_gdn, new_conv, new_re, new_im = [], [], [], [], [], []
    stream = off // nb
    for l in range(DEPTH):
        lng = lambda s: ln_g[l, s].reshape(1, D_MODEL)
        lnb = lambda s: ln_b[l, s].reshape(1, D_MODEL)
        x = _ffn_call(x, mod3, l, 0, off, *ffw[l][0], lng(0), lnb(0), cfg["ffn_bb"], cfg["ffn_tt"])
        i = l // 2
        if l % 2 == 0:
            ew = evw[i]
            conv0p = jnp.pad(conv0[i].astype(F32), ((0, 0), (SUBLANE - (GDN_CONV - 1), 0), (0, 0)))
            qkvn, z, bg, qa, ka, vv, ckv, kr, tail = _e1_call(
                x, mod3, l, off, ew, conv0p, rope, cfg["e1_bb"], cfg["e1_tt"], past is None)
            n_valid = min(T, GDN_C)
            if T < GDN_C:
                qkvn, z, bg = (_pad_rows(a, GDN_C) for a in (qkvn, z, bg))
            og, s_new = _gdn_call(qkvn, z, bg, gdn0[i].astype(F32), ew, cfg["gdn_bb"], n_valid)
            og = og[:, :T]
            if past is None:
                x = _mla_prompt_call(qa, ka, vv, og, x, mod3, l, off, ew["w_out"], lng(1), lnb(1), cfg["mla_tq"])
            else:
                x = _mla_sample_call(qa, past[0][i], past[1][i], ka, vv, og, x, mod3, l, off, ew, lng(1), lnb(1))
            new_gdn.append(s_new)
            new_conv.append(tail[:, SUBLANE - (GDN_CONV - 1):])
            new_ckv.append(ckv)
            new_kr.append(kr)
        else:
            sw = {k: v[i] for k, v in s5w.items()}
            x, h_re, h_im = _s5_call(x, mod, l, stream, s5re0[i].reshape(nb, S5_STATE).astype(F32),
                                     s5im0[i].reshape(nb, S5_STATE).astype(F32), sw, lng(1), lnb(1), cfg["s5_tt"])
            new_re.append(h_re.reshape(nb, S5_G, S5_P))
            new_im.append(h_im.reshape(nb, S5_G, S5_P))
        x = _ffn_call(x, mod3, l, 2, off, *ffw[l][1], lng(2), lnb(2), cfg["ffn_bb"], cfg["ffn_tt"])
    return (x, jnp.stack(new_ckv), jnp.stack(new_kr), jnp.stack(new_gdn), jnp.stack(new_conv),
            jnp.stack(new_re), jnp.stack(new_im))


def _config(nb, T):
    if T >= 512:
        return {"ffn_bb": 1, "ffn_tt": FF_ROWS, "e1_bb": 1, "e1_tt": 512, "gdn_bb": min(nb, 4), "mla_tq": 512,
                "s5_tt": 32}
    bb = max(1, min(nb, 512 // T))
    return {"ffn_bb": max(1, min(nb, FF_ROWS // T)), "ffn_tt": T, "e1_bb": bb, "e1_tt": T, "gdn_bb": min(nb, 4),
            "mla_tq": T, "s5_tt": min(T, 32)}


def kernel(x_prompt, x_sample, c_prompt, c_sample, cache_mla_latent, cache_mla_krope, state_gdn, state_gdn_conv, state_s5_re, state_s5_im, w_ada, b_ada, ln_g, ln_b, w_ff_in, w_ff_out, w_mix_in, gdn_conv_w, gdn_a_log, gdn_dt_bias, gdn_norm_w, mla_q_norm, mla_w_q_b, mla_kv_norm, mla_w_kv_b, w_mix_out, s5_lambda_re, s5_lambda_im, s5_log_dt, s5_b_re, s5_b_im, s5_c_re, s5_c_im, s5_d, s5_w_glu, s5_w_out):
    bp, sp, _ = x_prompt.shape
    bs, ss, _ = x_sample.shape
    assert bp == bs, "the modulation table is indexed in blocks of one stream's batch"

    mod = _ada_call(jnp.concatenate([c_prompt, c_sample], axis=0), w_ada, b_ada)
    mod3 = mod.reshape(DEPTH * (bp + bs), 1, -1)

    ffw = [[_prep_ffn(w_ff_in[l, s], w_ff_out[l, s]) for s in range(2)] for l in range(DEPTH)]
    evw = [_prep_even(w_mix_in[i], gdn_conv_w[i], gdn_a_log[i], gdn_dt_bias[i], gdn_norm_w[i], mla_q_norm[i],
                      mla_w_q_b[i], mla_kv_norm[i], mla_w_kv_b[i], w_mix_out[i]) for i in range(N_EVEN)]
    s5w = _prep_s5(s5_lambda_re, s5_lambda_im, s5_log_dt, s5_b_re, s5_b_im, s5_c_re, s5_c_im, s5_d,
                   s5_w_glu, s5_w_out)

    zeros = lambda *s: jnp.zeros(s, F32)
    outs_p = _run_trunk(
        x_prompt, 0, jnp.arange(sp, dtype=jnp.int32), mod, mod3, ffw, evw, s5w, ln_g, ln_b,
        zeros(N_EVEN, bp, GDN_H, GDN_DK, GDN_DV), zeros(N_EVEN, bp, GDN_CONV - 1, GDN_QKV),
        zeros(N_ODD, bp, S5_G, S5_P), zeros(N_ODD, bp, S5_G, S5_P), None, _config(bp, sp))

    n_past = cache_mla_latent.shape[2]
    past_kr = jnp.pad(cache_mla_krope, ((0, 0), (0, 0), (0, 0), (0, LANE - MLA_ROPE)))
    outs_s = _run_trunk(
        x_sample, bp, n_past + jnp.arange(ss, dtype=jnp.int32), mod, mod3, ffw, evw, s5w, ln_g, ln_b,
        state_gdn, state_gdn_conv, state_s5_re, state_s5_im, (cache_mla_latent, past_kr), _config(bs, ss))

    y_p, lat_p, kr_p, gdn_p, conv_p, re_p, im_p = outs_p
    y_s, lat_s, kr_s, gdn_s, conv_s, re_s, im_s = outs_s
    return (y_p, y_s, lat_p, kr_p, gdn_p, conv_p, re_p, im_p, lat_s, kr_s, gdn_s, conv_s, re_s, im_s)
```

```python
import functools
import math

import jax
import jax.numpy as jnp
from jax import lax
from jax.experimental import pallas as pl
from jax.experimental.pallas import tpu as pltpu

F32 = jnp.float32
BF16 = jnp.bfloat16

D_MODEL = 1024
DEPTH = 4
CHUNK = 64
N_EVEN = (DEPTH + 1) // 2
N_ODD = DEPTH // 2
DEEP_ALPHA = (2.0 * DEPTH) ** 0.25
EPS = 1e-6
D_FF = 2816

GDN_H = 4
GDN_DK = 128
GDN_DV = 128
GDN_CONV = 4
GDN_QKV = GDN_H * (2 * GDN_DK + GDN_DV)

MLA_H = 4
MLA_NOPE = 128
MLA_ROPE = 64
MLA_V = 128
MLA_Q_LORA = 256
MLA_KV_LORA = 128
MLA_SCALE = (MLA_NOPE + MLA_ROPE) ** -0.5
ROPE_THETA = 10000.0
LOG2_E = math.log2(math.e)
MLA_QSCALE = MLA_SCALE * LOG2_E

OFF_Z = GDN_QKV
OFF_B = OFF_Z + GDN_H * GDN_DV
OFF_A = OFF_B + GDN_H
OFF_QA = OFF_A + GDN_H
OFF_KVA = OFF_QA + MLA_Q_LORA

S5_M = 16
S5_G = D_MODEL // S5_M
S5_P = 64
S5_STATE = S5_G * S5_P
S5_KT = 256
S5_NKT = D_MODEL // S5_KT
S5_GPT = S5_KT // S5_M
S5_SPT = S5_GPT * S5_P
S5_SUB = 32

LANE = 128
SUBLANE = 8
VMEM_LIMIT = 56 * 1024 * 1024

FF_CHUNK = 256
FF_NCH = D_FF // FF_CHUNK
FF_SUB = 256
FF_ROWS = 1024
GDN_C = 128
MLA_DIAG = 256

E1_QKV = 0
E1_Z = GDN_QKV
E1_QA = E1_Z + GDN_H * GDN_DV
E1_CKV = E1_QA + MLA_Q_LORA
E1_KRA = E1_CKV + MLA_KV_LORA
E1_KRB = E1_KRA + LANE
E1_BA = E1_KRB + LANE
E1_COLS = E1_BA + LANE


def _cparams(sem):
    return pltpu.CompilerParams(dimension_semantics=sem, vmem_limit_bytes=VMEM_LIMIT)


def _dot(a, b):
    return jnp.dot(a, b, preferred_element_type=F32)


def _dot_nt(a, b):
    return lax.dot_general(a, b, (((1,), (1,)), ((), ())), preferred_element_type=F32)


def _sigmoid(x):
    return 1.0 / (1.0 + jnp.exp(-x))


def _silu(x):
    return x * _sigmoid(x)


def _layer_norm(y, g, b):
    mu = jnp.mean(y, axis=-1, keepdims=True)
    yc = y - mu
    var = jnp.mean(yc * yc, axis=-1, keepdims=True)
    return yc * lax.rsqrt(var + EPS) * g + b


def _rms_norm(y, g):
    return y * lax.rsqrt(jnp.mean(y * y, axis=-1, keepdims=True) + EPS) * g


def _split3(a):
    h = a.astype(BF16)
    r = a - h.astype(F32)
    m = r.astype(BF16)
    l = (r - m.astype(F32)).astype(BF16)
    return h, m, l


def _mm3(a, b):
    n = b.shape[1]
    ah = a.astype(BF16)
    al = (a - ah.astype(F32)).astype(BF16)
    bh = b.astype(BF16)
    bl = (b - bh.astype(F32)).astype(BF16)
    lhs = jnp.concatenate([ah, al], axis=1)
    rhs = jnp.concatenate([jnp.concatenate([bh, bl], axis=1),
                           jnp.concatenate([bh, jnp.zeros_like(bl)], axis=1)], axis=0)
    r = _dot(lhs, rhs)
    return r[:, :n] + r[:, n:]


def _ada_kernel(c_ref, w_ref, b_ref, o_ref):
    sc = _silu(c_ref[...]).astype(BF16)
    o_ref[0] = _dot(sc, w_ref[0].astype(BF16)) + b_ref[0]


def _ada_call(c_all, w_ada, b_ada):
    n = c_all.shape[0]
    ncol = w_ada.shape[-1] // D_MODEL
    return pl.pallas_call(
        _ada_kernel,
        grid=(DEPTH, ncol),
        in_specs=[
            pl.BlockSpec((n, D_MODEL), lambda l, j: (0, 0)),
            pl.BlockSpec((1, D_MODEL, D_MODEL), lambda l, j: (l, 0, j)),
            pl.BlockSpec((1, 1, D_MODEL), lambda l, j: (l, 0, j)),
        ],
        out_specs=pl.BlockSpec((1, n, D_MODEL), lambda l, j: (l, 0, j)),
        out_shape=jax.ShapeDtypeStruct((DEPTH, n, ncol * D_MODEL), F32),
        compiler_params=_cparams(("parallel", "parallel")),
        name="ada_mod",
    )(c_all, w_ada, b_ada.reshape(DEPTH, 1, -1))


def _mod_spec(mod3, layer, sub, part, off, bb):
    base = (layer * (mod3.shape[0] // DEPTH) + off) // bb
    col = sub * 3 + part
    return pl.BlockSpec((bb, 1, D_MODEL), lambda i, t: (base + i, 0, col))


def _ffn_kernel(x_ref, sh_ref, sc_ref, gt_ref, win_ref, wout_ref, lng_ref, lnb_ref, o_ref, ub_sc, h_sc,
                *, sub_rows):
    bb, tt, _ = x_ref.shape
    seg = min(tt, sub_rows)
    segs = [(b, t0) for b in range(bb) for t0 in range(0, tt, seg)]
    per_sub = sub_rows // seg

    for s in range(len(segs) // per_sub):
        rows = slice(s * sub_rows, (s + 1) * sub_rows)
        for k in range(s * per_sub, (s + 1) * per_sub):
            b, t0 = segs[k]
            ub_sc[k * seg:(k + 1) * seg] = (
                x_ref[b, t0:t0 + seg, :] * (1.0 + sc_ref[b]) + sh_ref[b]).astype(BF16)
        for c in range(FF_NCH):
            cols = slice(c * FF_CHUNK, (c + 1) * FF_CHUNK)
            gate = _dot(ub_sc[rows], win_ref[:, c * FF_CHUNK:(c + 1) * FF_CHUNK])
            up = _dot(ub_sc[rows], win_ref[:, D_FF + c * FF_CHUNK:D_FF + (c + 1) * FF_CHUNK])
            h_sc[rows, cols] = (_silu(gate) * up).astype(BF16)
        h = _dot(h_sc[rows], wout_ref[...])
        for j in range(per_sub):
            b, t0 = segs[s * per_sub + j]
            y = DEEP_ALPHA * x_ref[b, t0:t0 + seg, :] + 0.5 * (1.0 + gt_ref[b]) * h[j * seg:(j + 1) * seg]
            o_ref[b, t0:t0 + seg, :] = _layer_norm(y, lng_ref[...], lnb_ref[...])


def _ffn_call(x, mod3, layer, sub, off, win_r, wout_r, lng, lnb, bb, tt):
    nb, T, _ = x.shape
    const2 = lambda i, t: (0, 0)
    rows = bb * tt
    sub_rows = min(FF_SUB, rows)
    assert rows % sub_rows == 0 and (tt % sub_rows == 0 or sub_rows % tt == 0)
    tok = pl.BlockSpec((bb, tt, D_MODEL), lambda i, t: (i, t, 0))
    return pl.pallas_call(
        functools.partial(_ffn_kernel, sub_rows=sub_rows),
        grid=(nb // bb, T // tt),
        in_specs=[
            tok,
            _mod_spec(mod3, layer, sub, 0, off, bb),
            _mod_spec(mod3, layer, sub, 1, off, bb),
            _mod_spec(mod3, layer, sub, 2, off, bb),
            pl.BlockSpec(win_r.shape, const2, pipeline_mode=pl.Buffered(1)),
            pl.BlockSpec(wout_r.shape, const2, pipeline_mode=pl.Buffered(1)),
            pl.BlockSpec((1, D_MODEL), const2),
            pl.BlockSpec((1, D_MODEL), const2),
        ],
        out_specs=tok,
        out_shape=jax.ShapeDtypeStruct(x.shape, F32),
        scratch_shapes=[pltpu.VMEM((rows, D_MODEL), BF16), pltpu.VMEM((rows, D_FF), BF16)],
        compiler_params=_cparams(("parallel", "parallel")),
        name="ffn",
    )(x, mod3, mod3, mod3, win_r, wout_r, lng, lnb)


def _e1_kernel(x_ref, sh_ref, sc_ref, w_ref, conv0_ref, convw_ref, cos_ref, sin_ref, cost_ref, sint_ref,
               qn_ref, kvn_ref, wqa_ref, wqb_ref, wkn_ref, wv_ref,
               qkvn_ref, z_ref, bg_ref, qa_ref, ka_ref, vv_ref, ckv_ref, kr_ref, tail_ref, tail_sc,
               *, feature_major):
    bb, tt, _ = x_ref.shape
    rows = bb * tt
    assert bb == 1 or not feature_major

    @pl.when(pl.program_id(1) == 0)
    def _():
        tail_sc[...] = conv0_ref[...]

    u = x_ref[...] * (1.0 + sc_ref[...]) + sh_ref[...]
    ub = u.reshape(rows, D_MODEL).astype(BF16)

    pre = _dot(ub, w_ref[:, E1_QKV:E1_Z])
    for b in range(bb):
        pre_b = pre[b * tt:(b + 1) * tt]
        hist = jnp.concatenate([tail_sc[b], pre_b], axis=0)
        y = pre_b * convw_ref[GDN_CONV - 1:GDN_CONV, :]
        for lag in range(1, GDN_CONV):
            w_row = convw_ref[GDN_CONV - 1 - lag:GDN_CONV - lag, :]
            y = y + pltpu.roll(hist, lag, 0)[SUBLANE:] * w_row
        tail_sc[b] = pre_b[tt - SUBLANE:]
        y = _silu(y)
        for h in range(GDN_H):
            qh = y[:, h * GDN_DK:(h + 1) * GDN_DK]
            qkvn_ref[b, :, h * GDN_DK:(h + 1) * GDN_DK] = (
                qh * lax.rsqrt(jnp.sum(qh * qh, axis=-1, keepdims=True) + EPS) * (GDN_DK ** -0.5))
            ko = GDN_H * GDN_DK + h * GDN_DK
            kh = y[:, ko:ko + GDN_DK]
            qkvn_ref[b, :, ko:ko + GDN_DK] = kh * lax.rsqrt(jnp.sum(kh * kh, axis=-1, keepdims=True) + EPS)
        vo = 2 * GDN_H * GDN_DK
        qkvn_ref[b, :, vo:] = y[:, vo:]
    tail_ref[...] = tail_sc[...]

    z_ref[...] = _dot(ub, w_ref[:, E1_Z:E1_QA]).reshape(bb, tt, -1)
    bg_ref[...] = _dot(ub, w_ref[:, E1_BA:E1_COLS]).reshape(bb, tt, -1)

    cos = cos_ref[...]
    sin = sin_ref[...]
    if bb > 1:
        cos = jnp.concatenate([cos] * bb, axis=0)
        sin = jnp.concatenate([sin] * bb, axis=0)

    cq = _rms_norm(_dot(ub, w_ref[:, E1_QA:E1_CKV]), qn_ref[...])
    if feature_major:
        cq_t = cq.T.astype(BF16)
        qa = _dot(wqa_ref[...], cq_t)
        qb = _dot(wqb_ref[...], cq_t)
        for h in range(MLA_H):
            o = 2 * LANE * h
            qa_ref[0, o:o + LANE, :] = (qa[o:o + LANE] * MLA_QSCALE).astype(BF16)
            rot = qa[o + LANE:o + 2 * LANE] * cost_ref[...] + qb[h * LANE:(h + 1) * LANE] * sint_ref[...]
            qa_ref[0, o + LANE:o + 2 * LANE, :] = (rot * MLA_QSCALE).astype(BF16)
    else:
        cqb = cq.astype(BF16)
        qa = _dot(cqb, wqa_ref[...])
        qb = _dot(cqb, wqb_ref[...])
        for h in range(MLA_H):
            o = 2 * LANE * h
            qa_ref[:, :, o:o + LANE] = (qa[:, o:o + LANE] * MLA_QSCALE).astype(BF16).reshape(bb, tt, LANE)
            rot = qa[:, o + LANE:o + 2 * LANE] * cos + qb[:, h * LANE:(h + 1) * LANE] * sin
            qa_ref[:, :, o + LANE:o + 2 * LANE] = (rot * MLA_QSCALE).astype(BF16).reshape(bb, tt, LANE)

    ckv = _rms_norm(_dot(ub, w_ref[:, E1_CKV:E1_KRA]), kvn_ref[...])
    ckv_ref[...] = ckv.reshape(bb, tt, -1)
    kr = _dot(ub, w_ref[:, E1_KRA:E1_KRB]) * cos + _dot(ub, w_ref[:, E1_KRB:E1_BA]) * sin
    kr_ref[...] = kr[:, :MLA_ROPE].reshape(bb, tt, -1)
    cb = ckv.astype(BF16)
    krb = kr.astype(BF16).reshape(bb, tt, LANE)
    kn = _dot(cb, wkn_ref[...])
    for h in range(MLA_H):
        o = 2 * LANE * h
        ka_ref[:, :, o:o + LANE] = kn[:, h * LANE:(h + 1) * LANE].astype(BF16).reshape(bb, tt, LANE)
        ka_ref[:, :, o + LANE:o + 2 * LANE] = krb
    if feature_major:
        vv_ref[0] = _dot(wv_ref[...], ckv.T.astype(BF16)).astype(BF16)
    else:
        vv_ref[...] = _dot(cb, wv_ref[...]).astype(BF16).reshape(bb, tt, -1)


def _e1_call(x, mod3, layer, off, ew, conv0p, rope, bb, tt, feature_major):
    nb, T, _ = x.shape
    tok = lambda i, t: (i, t, 0)
    const2 = lambda i, t: (0, 0)

    def full(a):
        return pl.BlockSpec(a.shape, const2)

    out_dims = [(GDN_QKV, F32), (GDN_H * GDN_DV, F32), (LANE, F32), (MLA_H * 2 * LANE, BF16),
                (MLA_H * 2 * LANE, BF16), (MLA_H * MLA_V, BF16), (MLA_KV_LORA, F32), (MLA_ROPE, F32)]
    out_shape = [jax.ShapeDtypeStruct((nb, T, d), dt) for d, dt in out_dims]
    out_specs = [pl.BlockSpec((bb, tt, d), tok) for d, _ in out_dims]
    wqa, wqb, wv = ew["wqa"], ew["wqb"], ew["wv"]
    if feature_major:
        wqa, wqb, wv = ew["wqa_t"], ew["wqb_t"], ew["wv_t"]
        for idx in (3, 5):
            d = out_dims[idx][0]
            out_shape[idx] = jax.ShapeDtypeStruct((nb, d, T), BF16)
            out_specs[idx] = pl.BlockSpec((bb, d, tt), lambda i, t: (i, 0, t))
    out_shape.append(jax.ShapeDtypeStruct((nb, SUBLANE, GDN_QKV), F32))
    out_specs.append(pl.BlockSpec((bb, SUBLANE, GDN_QKV), lambda i, t: (i, 0, 0)))
    cos_t, sin_t = rope["rows"]
    cos_f, sin_f = rope["cols"]
    return pl.pallas_call(
        functools.partial(_e1_kernel, feature_major=feature_major),
        grid=(nb // bb, T // tt),
        in_specs=[
            pl.BlockSpec((bb, tt, D_MODEL), tok),
            _mod_spec(mod3, layer, 1, 0, off, bb),
            _mod_spec(mod3, layer, 1, 1, off, bb),
            full(ew["w_e1"]),
            pl.BlockSpec((bb, SUBLANE, GDN_QKV), lambda i, t: (i, 0, 0)),
            full(ew["conv_w"]),
            pl.BlockSpec((tt, LANE), lambda i, t: (t, 0)),
            pl.BlockSpec((tt, LANE), lambda i, t: (t, 0)),
            pl.BlockSpec((LANE, tt), lambda i, t: (0, t)),
            pl.BlockSpec((LANE, tt), lambda i, t: (0, t)),
            full(ew["q_norm"]), full(ew["kv_norm"]),
            full(wqa), full(wqb), full(ew["wkn"]), full(wv),
        ],
        out_specs=out_specs,
        out_shape=out_shape,
        scratch_shapes=[pltpu.VMEM((bb, SUBLANE, GDN_QKV), F32)],
        compiler_params=_cparams(("parallel", "arbitrary")),
        name="mix_proj",
    )(x, mod3, mod3, ew["w_e1"], conv0p, ew["conv_w"], cos_t, sin_t, cos_f, sin_f, ew["q_norm"], ew["kv_norm"],
      wqa, wqb, ew["wkn"], wv)


def _gdn_kernel(qkvn_ref, z_ref, bg_ref, s0_ref, alog_ref, dtb_ref, nw_ref, o_ref, s_ref, s_sc, *, n_valid):
    bb = qkvn_ref.shape[0]
    C = GDN_C
    units = [(b, h) for b in range(bb) for h in range(GDN_H)]

    @pl.when(pl.program_id(1) == 0)
    def _():
        s_sc[...] = s0_ref[...]

    row = lax.broadcasted_iota(jnp.int32, (C, C), 0)
    col = lax.broadcasted_iota(jnp.int32, (C, C), 1)
    incl = row >= col
    strict = row > col
    tri = jnp.where(incl, 1.0, 0.0).astype(BF16)
    eye = jnp.where(row == col, 1.0, 0.0)

    beta_all, G_all, G_t = [], [], []
    for b in range(bb):
        bg = bg_ref[b]
        beta_all.append(_sigmoid(bg))
        xs = bg + dtb_ref[...]
        softplus = jnp.maximum(xs, 0.0) + jnp.log(1.0 + jnp.exp(-jnp.abs(xs)))
        g_all = -jnp.exp(alog_ref[...]) * softplus
        if n_valid < C:
            g_all = jnp.where(row < n_valid, g_all, 0.0)
        g3 = _dot(tri, jnp.concatenate(_split3(g_all), axis=1))
        G_all.append(g3[:, :LANE] + g3[:, LANE:2 * LANE] + g3[:, 2 * LANE:])
        G_t.append(G_all[b].T)

    def load(b, h, part):
        o = (part * GDN_H + h) * GDN_DK
        return qkvn_ref[b, :, o:o + GDN_DK]

    q = [load(b, h, 0) for b, h in units]
    k = [load(b, h, 1) for b, h in units]
    v = [load(b, h, 2) for b, h in units]
    kb = [x.astype(BF16) for x in k]
    beta = [beta_all[b][:, h:h + 1] for b, h in units]
    Gc = [G_all[b][:, GDN_H + h:GDN_H + h + 1] for b, h in units]
    Gr = [G_t[b][GDN_H + h:GDN_H + h + 1, :] for b, h in units]
    decay = [jnp.where(incl, jnp.exp(jnp.where(incl, c - r, 0.0)), 0.0) for c, r in zip(Gc, Gr)]
    kq = [_dot_nt(jnp.concatenate([y, x.astype(BF16)], axis=0), y) for x, y in zip(q, kb)]
    lower = [jnp.where(strict, bt * d * m[:C], 0.0) for bt, d, m in zip(beta, decay, kq)]
    inv = [eye - jnp.where(row // 2 == col // 2, m, 0.0) for m in lower]
    b = 2
    while b < C:
        pair_off = (row // (2 * b) == col // (2 * b)) & (row // b != col // b)
        inv = [a - _mm3(a, _mm3(jnp.where(pair_off, m, 0.0), a)) for a, m in zip(inv, lower)]
        b *= 2
    gam = [jnp.exp(c) for c in Gc]
    sol_v = [_mm3(a, bt * x) for a, bt, x in zip(inv, beta, v)]
    sol_k = [_mm3(a, (bt * g) * x).astype(BF16) for a, bt, g, x in zip(inv, beta, gam, k)]
    attn = [(m[C:] * d).astype(BF16) for m, d in zip(kq, decay)]
    qg = [(g * x).astype(BF16) for g, x in zip(gam, q)]
    g_end = [c[C - 1:C, :] for c in Gc]
    kd = [(jnp.exp(e - c) * x).T.astype(BF16) for e, c, x in zip(g_end, Gc, k)]

    s0 = [s_sc[b, h] for b, h in units]
    with_s0 = [_dot(jnp.concatenate([w, x], axis=0), s.astype(BF16)) for w, x, s in zip(sol_k, qg, s0)]
    ub = [(a - r[:C]).astype(BF16) for a, r in zip(sol_v, with_s0)]
    with_u = [_dot(jnp.concatenate([a, x], axis=0), y) for a, x, y in zip(attn, kd, ub)]
    o = [r[C:] + t[:C] for r, t in zip(with_s0, with_u)]
    s1 = [jnp.exp(e) * s + t[C:] for e, s, t in zip(g_end, s0, with_u)]
    for i, (b, h) in enumerate(units):
        s_sc[b, h] = s1[i]
        zh = z_ref[b, :, h * GDN_DV:(h + 1) * GDN_DV]
        o_ref[b, :, h * GDN_DV:(h + 1) * GDN_DV] = (_rms_norm(o[i], nw_ref[...]) * _silu(zh)).astype(BF16)
    s_ref[...] = s_sc[...]


def _gdn_call(qkvn, z, bg, s0, ew, bb, n_valid):
    nb, T, _ = qkvn.shape
    tok = lambda i, c: (i, c, 0)
    const2 = lambda i, c: (0, 0)
    st = lambda i, c: (i, 0, 0, 0)
    return pl.pallas_call(
        functools.partial(_gdn_kernel, n_valid=n_valid),
        grid=(nb // bb, T // GDN_C),
        in_specs=[
            pl.BlockSpec((bb, GDN_C, GDN_QKV), tok),
            pl.BlockSpec((bb, GDN_C, GDN_H * GDN_DV), tok),
            pl.BlockSpec((bb, GDN_C, LANE), tok),
            pl.BlockSpec((bb, GDN_H, GDN_DK, GDN_DV), st),
            pl.BlockSpec((1, LANE), const2),
            pl.BlockSpec((1, LANE), const2),
            pl.BlockSpec((1, GDN_DV), const2),
        ],
        out_specs=[pl.BlockSpec((bb, GDN_C, GDN_H * GDN_DV), tok),
                   pl.BlockSpec((bb, GDN_H, GDN_DK, GDN_DV), st)],
        out_shape=[jax.ShapeDtypeStruct((nb, T, GDN_H * GDN_DV), BF16),
                   jax.ShapeDtypeStruct((nb, GDN_H, GDN_DK, GDN_DV), F32)],
        scratch_shapes=[pltpu.VMEM((bb, GDN_H, GDN_DK, GDN_DV), F32)],
        compiler_params=_cparams(("parallel", "arbitrary")),
        name="gdn",
    )(qkvn, z, bg, s0, ew["alog"], ew["dtb"], ew["norm_w"])


def _attn_init(m_sc, l_sc, acc_sc):
    m_sc[...] = jnp.full_like(m_sc, -jnp.inf)
    l_sc[...] = jnp.zeros_like(l_sc)
    acc_sc[...] = jnp.zeros_like(acc_sc)


def _mix_out(og, heads_out, x_ref, gt_ref, wout_ref, lng_ref, lnb_ref, o_ref):
    mix = _dot(jnp.concatenate([og] + heads_out, axis=-1), wout_ref[...])
    y = DEEP_ALPHA * x_ref[0] + (1.0 + gt_ref[0]) * mix
    o_ref[0] = _layer_norm(y, lng_ref[...], lnb_ref[...])


def _mla_prompt_kernel(qa_ref, ka_ref, vv_ref, og_ref, x_ref, gt_ref, wout_ref, lng_ref, lnb_ref, o_ref,
                       m_sc, l_sc, acc_sc):
    tq = qa_ref.shape[2]
    i = pl.program_id(1)
    _attn_init(m_sc, l_sc, acc_sc)
    heads = range(MLA_H)
    def block(ks, nk, q0, nq, q_off):
        qs = slice(q0, q0 + nq)
        s = [_dot(ka_ref[0, pl.ds(ks, nk), 2 * LANE * h:2 * LANE * (h + 1)],
                  qa_ref[0, 2 * LANE * h:2 * LANE * (h + 1), qs]) for h in heads]
        if q_off is not None:
            vis = ((q_off + lax.broadcasted_iota(jnp.int32, (nk, nq), 1)) // CHUNK
                   >= lax.broadcasted_iota(jnp.int32, (nk, nq), 0) // CHUNK)
            s = [jnp.where(vis, x, -jnp.inf) for x in s]
        m_prev = [m_sc[h, :, qs] for h in heads]
        m_new = [jnp.maximum(m, jnp.max(x, axis=0, keepdims=True)) for m, x in zip(m_prev, s)]
        p = [jnp.exp2(x - m) for x, m in zip(s, m_new)]
        alpha = [jnp.exp2(a - b) for a, b in zip(m_prev, m_new)]
        pv = [_dot(vv_ref[0, MLA_V * h:MLA_V * (h + 1), pl.ds(ks, nk)], p[h].astype(BF16)) for h in heads]
        for h in heads:
            l_sc[h, :, qs] = alpha[h] * l_sc[h, :, qs] + jnp.sum(p[h], axis=0, keepdims=True)
            acc_sc[h, :, qs] = alpha[h] * acc_sc[h, :, qs] + pv[h]
            m_sc[h, :, qs] = m_new[h]

    def body(j, carry):
        block(pl.multiple_of(j * tq, tq), tq, 0, tq, None)
        return carry

    lax.fori_loop(0, i, body, 0)
    sub = min(tq, MLA_DIAG)
    d0 = pl.multiple_of(i * tq, tq)
    for kb in range(tq // sub):
        ks = pl.multiple_of(d0 + kb * sub, sub)
        block(ks, sub, kb * sub, sub, 0)
        if (kb + 1) * sub < tq:
            block(ks, sub, (kb + 1) * sub, tq - (kb + 1) * sub, None)
    heads_out = [(acc_sc[h] / l_sc[h]).T.astype(BF16) for h in heads]
    _mix_out(og_ref[0], heads_out, x_ref, gt_ref, wout_ref, lng_ref, lnb_ref, o_ref)


def _mla_sample_kernel(qa_ref, pckv_ref, pkr_ref, ka_ref, vv_ref, og_ref, x_ref, gt_ref, wknt_ref, wv_ref,
                       wout_ref, lng_ref, lnb_ref, o_ref):
    ckv = pckv_ref[0]
    kn_t = _dot(wknt_ref[...], ckv.T.astype(BF16)).astype(BF16)
    kr_t = pkr_ref[0].T.astype(BF16)
    vp = _dot(ckv.astype(BF16), wv_ref[...]).astype(BF16)
    heads_out = []
    for h in range(MLA_H):
        q = qa_ref[0, :, 2 * LANE * h:2 * LANE * (h + 1)]
        k_t = jnp.concatenate([kn_t[h * MLA_NOPE:(h + 1) * MLA_NOPE], kr_t], axis=0)
        s_p = _dot(q, k_t)
        s_n = _dot_nt(q, ka_ref[0, :, 2 * LANE * h:2 * LANE * (h + 1)])
        m = jnp.maximum(jnp.max(s_p, axis=-1, keepdims=True), jnp.max(s_n, axis=-1, keepdims=True))
        p_p = jnp.exp2(s_p - m)
        p_n = jnp.exp2(s_n - m)
        l = jnp.sum(p_p, axis=-1, keepdims=True) + jnp.sum(p_n, axis=-1, keepdims=True)
        o = (_dot(p_p.astype(BF16), vp[:, MLA_V * h:MLA_V * (h + 1)])
             + _dot(p_n.astype(BF16), vv_ref[0, :, MLA_V * h:MLA_V * (h + 1)]))
        heads_out.append((o / l).astype(BF16))
    _mix_out(og_ref[0], heads_out, x_ref, gt_ref, wout_ref, lng_ref, lnb_ref, o_ref)


def _mla_prompt_call(qa, ka, vv, og, x, mod3, layer, off, w_out, lng, lnb, tq):
    nb, T, _ = x.shape
    tok = lambda b, i: (b, i, 0)
    seq = lambda b, i: (b, 0, 0)
    const2 = lambda b, i: (0, 0)
    return pl.pallas_call(
        _mla_prompt_kernel,
        grid=(nb, T // tq),
        in_specs=[
            pl.BlockSpec((1, qa.shape[1], tq), lambda b, i: (b, 0, i)),
            pl.BlockSpec((1,) + ka.shape[1:], seq),
            pl.BlockSpec((1,) + vv.shape[1:], seq),
            pl.BlockSpec((1, tq, og.shape[-1]), tok),
            pl.BlockSpec((1, tq, D_MODEL), tok),
            _mod_spec(mod3, layer, 1, 2, off, 1),
            pl.BlockSpec(w_out.shape, const2),
            pl.BlockSpec((1, D_MODEL), const2),
            pl.BlockSpec((1, D_MODEL), const2),
        ],
        out_specs=pl.BlockSpec((1, tq, D_MODEL), tok),
        out_shape=jax.ShapeDtypeStruct(x.shape, F32),
        scratch_shapes=[pltpu.VMEM((MLA_H, 1, tq), F32), pltpu.VMEM((MLA_H, 1, tq), F32),
                        pltpu.VMEM((MLA_H, MLA_V, tq), F32)],
        compiler_params=_cparams(("parallel", "arbitrary")),
        name="mla_prompt",
    )(qa, ka, vv, og, x, mod3, w_out, lng, lnb)


def _mla_sample_call(qa, past_ckv, past_kr128, ka, vv, og, x, mod3, layer, off, ew, lng, lnb):
    nb, T, _ = x.shape
    seq = lambda b, i: (b, 0, 0)
    const2 = lambda b, i: (0, 0)

    def whole(a):
        return pl.BlockSpec((1,) + a.shape[1:], seq)

    def full(a):
        return pl.BlockSpec(a.shape, const2)

    return pl.pallas_call(
        _mla_sample_kernel,
        grid=(nb, 1),
        in_specs=[whole(qa), whole(past_ckv), whole(past_kr128), whole(ka), whole(vv), whole(og), whole(x),
                  _mod_spec(mod3, layer, 1, 2, off, 1),
                  full(ew["wkn_t"]), full(ew["wv"]), full(ew["w_out"]), full(lng), full(lnb)],
        out_specs=whole(x),
        out_shape=jax.ShapeDtypeStruct(x.shape, F32),
        compiler_params=_cparams(("parallel", "arbitrary")),
        name="mla_sample",
    )(qa, past_ckv, past_kr128, ka, vv, og, x, mod3, ew["wkn_t"], ew["wv"], ew["w_out"], lng, lnb)


def _s5prep_kernel(lre_ref, lim_ref, ldt_ref, br_ref, bi_ref, abre_ref, abim_ref, bbre_ref, bbim_ref):
    dt = jnp.exp(ldt_ref[0])
    lr = jnp.minimum(lre_ref[0], -1e-4)
    li = lim_ref[0]
    mag = jnp.exp(lr * dt)
    ab_re = mag * jnp.cos(li * dt)
    ab_im = mag * jnp.sin(li * dt)
    den = lr * lr + li * li
    coef_re = ((ab_re - 1.0) * lr + ab_im * li) / den
    coef_im = (ab_im * lr - (ab_re - 1.0) * li) / den
    br = br_ref[0]
    bi = bi_ref[0]
    abre_ref[0] = ab_re
    abim_ref[0] = ab_im
    bbre_ref[0] = coef_re * br - coef_im * bi
    bbim_ref[0] = coef_re * bi + coef_im * br


def _s5prep_call(lre_x, lim_x, ldt_x, br_t, bi_t):
    n = lre_x.shape[0]
    blk = pl.BlockSpec((1, D_MODEL, S5_P), lambda i: (i, 0, 0))
    return pl.pallas_call(
        _s5prep_kernel,
        grid=(n,),
        in_specs=[blk, blk, pl.BlockSpec((1, D_MODEL, 1), lambda i: (i, 0, 0)), blk, blk],
        out_specs=[blk] * 4,
        out_shape=[jax.ShapeDtypeStruct((n, D_MODEL, S5_P), F32)] * 4,
        compiler_params=_cparams(("parallel",)),
        name="s5_discretise",
    )(lre_x, lim_x, ldt_x, br_t, bi_t)


def _s5_kernel(x_ref, sh_ref, sc_ref, gt_ref, h0re_ref, h0im_ref, abre_ref, abim_ref, bb_ref, cre_ref, cim_ref,
               dskip_ref, wglu_ref, wout_ref, lng_ref, lnb_ref, o_ref, hre_ref, him_ref,
               hre_sc, him_sc, xt_sc, ub_sc, st_sc, y_sc):
    nb, tt, _ = x_ref.shape
    n_sub, ts = xt_sc.shape[0], xt_sc.shape[1]
    rows = ts * nb
    half = S5_SPT // 2

    @pl.when(pl.program_id(0) == 0)
    def _():
        hre_sc[...] = h0re_ref[...]
        him_sc[...] = h0im_ref[...]

    for s in range(n_sub):
        frames = slice(s * ts, (s + 1) * ts)
        xt_sc[s] = pltpu.einshape("btd->tbd", x_ref[:, frames, :])

        def modulated():
            return (xt_sc[s] * (1.0 + sc_ref[...]) + sh_ref[...]).reshape(rows, D_MODEL)

        ub_sc[s] = modulated().astype(BF16)
        for kt in range(S5_NKT):
            st_sc[s, kt] = _dot(ub_sc[s, :, kt * S5_KT:(kt + 1) * S5_KT], bb_ref[kt])
        for kt in range(S5_NKT):
            for lc in range(2):
                so = kt * S5_SPT + lc * half
                re_cols = slice(lc * half, (lc + 1) * half)
                im_cols = slice(S5_SPT + lc * half, S5_SPT + (lc + 1) * half)
                a_re = jnp.broadcast_to(abre_ref[:, so:so + half], (nb, half))
                a_im = jnp.broadcast_to(abim_ref[:, so:so + half], (nb, half))
                h_re = hre_sc[:, so:so + half]
                h_im = him_sc[:, so:so + half]
                for t in range(ts):
                    frame = slice(t * nb, (t + 1) * nb)
                    n_re = a_re * h_re - a_im * h_im + st_sc[s, kt, frame, re_cols]
                    n_im = a_re * h_im + a_im * h_re + st_sc[s, kt, frame, im_cols]
                    st_sc[s, kt, frame, re_cols] = n_re
                    st_sc[s, kt, frame, im_cols] = n_im
                    h_re, h_im = n_re, n_im
                hre_sc[:, so:so + half] = h_re
                him_sc[:, so:so + half] = h_im
            y_sc[s, :, kt * S5_KT:(kt + 1) * S5_KT] = (
                _dot(st_sc[s, kt, :, :S5_SPT].astype(BF16), cre_ref[kt])
                - _dot(st_sc[s, kt, :, S5_SPT:].astype(BF16), cim_ref[kt]))

        y = y_sc[s] + dskip_ref[...] * modulated()
        yg = 0.5 * y * (1.0 + jnp.tanh(math.sqrt(2.0 / math.pi) * (y + 0.044715 * (y * y * y))))
        glu = _dot(yg.astype(BF16), wglu_ref[...])
        mix = _dot((yg * _sigmoid(glu)).astype(BF16), wout_ref[...])
        out = DEEP_ALPHA * xt_sc[s] + (1.0 + gt_ref[...]) * mix.reshape(ts, nb, D_MODEL)
        o_ref[:, frames, :] = pltpu.einshape("tbd->btd", _layer_norm(out, lng_ref[...], lnb_ref[...]))

    @pl.when(pl.program_id(0) == pl.num_programs(0) - 1)
    def _():
        hre_ref[...] = hre_sc[...]
        him_ref[...] = him_sc[...]


def _s5_call(x, mod, layer, stream, h0re, h0im, sw, lng, lnb, tt):
    nb, T, _ = x.shape
    ts = min(tt, S5_SUB)
    n_sub = tt // ts
    rows = ts * nb
    const2 = lambda t: (0, 0)
    const3 = lambda t: (0, 0, 0)

    def mspec(part):
        return pl.BlockSpec((1, nb, D_MODEL), lambda t: (layer, stream, 3 + part))

    def full(a):
        return pl.BlockSpec(a.shape, const2 if a.ndim == 2 else const3, pipeline_mode=pl.Buffered(1))

    return pl.pallas_call(
        _s5_kernel,
        grid=(T // tt,),
        in_specs=[
            pl.BlockSpec((nb, tt, D_MODEL), lambda t: (0, t, 0)),
            mspec(0), mspec(1), mspec(2),
            full(h0re), full(h0im), full(sw["ab_re"]), full(sw["ab_im"]),
            full(sw["bb"]), full(sw["c_re"]), full(sw["c_im"]), full(sw["d_skip"]),
            full(sw["w_glu"]), full(sw["w_out"]), full(lng), full(lnb),
        ],
        out_specs=[pl.BlockSpec((nb, tt, D_MODEL), lambda t: (0, t, 0)),
                   pl.BlockSpec((nb, S5_STATE), const2), pl.BlockSpec((nb, S5_STATE), const2)],
        out_shape=[jax.ShapeDtypeStruct(x.shape, F32),
                   jax.ShapeDtypeStruct((nb, S5_STATE), F32), jax.ShapeDtypeStruct((nb, S5_STATE), F32)],
        scratch_shapes=[pltpu.VMEM((nb, S5_STATE), F32), pltpu.VMEM((nb, S5_STATE), F32),
                        pltpu.VMEM((n_sub, ts, nb, D_MODEL), F32), pltpu.VMEM((n_sub, rows, D_MODEL), BF16),
                        pltpu.VMEM((n_sub, S5_NKT, rows, 2 * S5_SPT), F32),
                        pltpu.VMEM((n_sub, rows, D_MODEL), F32)],
        compiler_params=_cparams(("arbitrary",)),
        name="s5_mixer",
    )(x, mod, mod, mod, h0re, h0im, sw["ab_re"], sw["ab_im"], sw["bb"], sw["c_re"], sw["c_im"],
      sw["d_skip"], sw["w_glu"], sw["w_out"], lng, lnb)


def _prep_ffn(w_in, w_out):
    return w_in.astype(BF16), w_out.astype(BF16)


def _prep_even(w_in, conv_w, a_log, dt_bias, norm_w, q_norm, w_q_b, kv_norm, w_kv_b, w_out):
    half = MLA_ROPE // 2
    zcol = lambda n: jnp.zeros((w_in.shape[0], n), w_in.dtype)
    k1 = w_in[:, OFF_KVA + MLA_KV_LORA:OFF_KVA + MLA_KV_LORA + half]
    k2 = w_in[:, OFF_KVA + MLA_KV_LORA + half:OFF_KVA + MLA_KV_LORA + MLA_ROPE]
    w_e1 = jnp.concatenate([
        w_in[:, :OFF_B],
        w_in[:, OFF_QA:OFF_KVA + MLA_KV_LORA],
        k1, k2, zcol(LANE - MLA_ROPE),
        -k2, k1, zcol(LANE - MLA_ROPE),
        w_in[:, OFF_B:OFF_QA], zcol(LANE - 2 * GDN_H),
    ], axis=1).astype(BF16)
    zq = jnp.zeros((MLA_Q_LORA, LANE - MLA_ROPE), w_q_b.dtype)
    qa_cols, qb_cols = [], []
    for h in range(MLA_H):
        o = h * (MLA_NOPE + MLA_ROPE)
        r1 = w_q_b[:, o + MLA_NOPE:o + MLA_NOPE + half]
        r2 = w_q_b[:, o + MLA_NOPE + half:o + MLA_NOPE + MLA_ROPE]
        qa_cols += [w_q_b[:, o:o + MLA_NOPE], r1, r2, zq]
        qb_cols += [-r2, r1, zq]
    kv = w_kv_b.reshape(MLA_KV_LORA, MLA_H, MLA_NOPE + MLA_V)
    pad = lambda v: jnp.pad(v.astype(F32), (GDN_H, LANE - 2 * GDN_H)).reshape(1, LANE)
    return {
        "w_e1": w_e1,
        "conv_w": conv_w.astype(F32),
        "alog": pad(a_log), "dtb": pad(dt_bias),
        "norm_w": norm_w.reshape(1, -1).astype(F32),
        "q_norm": q_norm.reshape(1, -1).astype(F32),
        "kv_norm": kv_norm.reshape(1, -1).astype(F32),
        "wqa": jnp.concatenate(qa_cols, axis=1).astype(BF16),
        "wqb": jnp.concatenate(qb_cols, axis=1).astype(BF16),
        "wqa_t": jnp.concatenate(qa_cols, axis=1).T.astype(BF16),
        "wqb_t": jnp.concatenate(qb_cols, axis=1).T.astype(BF16),
        "wv_t": kv[:, :, MLA_NOPE:].reshape(MLA_KV_LORA, -1).T.astype(BF16),
        "wkn": kv[:, :, :MLA_NOPE].reshape(MLA_KV_LORA, -1).astype(BF16),
        "wkn_t": kv[:, :, :MLA_NOPE].reshape(MLA_KV_LORA, -1).T.astype(BF16),
        "wv": kv[:, :, MLA_NOPE:].reshape(MLA_KV_LORA, -1).astype(BF16),
        "w_out": w_out.astype(BF16),
    }


def _rope_tables(pos):
    half = MLA_ROPE // 2
    inv = ROPE_THETA ** (-jnp.arange(half, dtype=F32) / half)
    ang = pos.astype(F32)[:, None] * inv[None, :]
    z = jnp.zeros((pos.shape[0], LANE - MLA_ROPE), F32)
    cos, sin = jnp.cos(ang), jnp.sin(ang)
    cos, sin = jnp.concatenate([cos, cos, z], axis=1), jnp.concatenate([sin, sin, z], axis=1)
    return {"rows": (cos, sin), "cols": (cos.T, sin.T)}


def _prep_s5(lam_re, lam_im, log_dt, b_re, b_im, c_re, c_im, d_skip, w_glu, w_out):
    n = lam_re.shape[0]
    rep = lambda a: jnp.repeat(a.astype(F32), S5_M, axis=1)
    to_gmp = lambda a: jnp.transpose(a.astype(F32), (0, 1, 3, 2)).reshape(n, D_MODEL, S5_P)
    ab_re, ab_im, bb_re, bb_im = _s5prep_call(
        rep(lam_re), rep(lam_im), jnp.repeat(log_dt.astype(F32), S5_M, axis=1)[..., None],
        to_gmp(b_re), to_gmp(b_im))
    eye = jnp.eye(S5_GPT, dtype=F32)

    def embed_in(bb):
        t = bb.reshape(n, S5_NKT, S5_GPT, S5_M, S5_P)
        return jnp.einsum("nkgmp,gh->nkgmhp", t, eye).reshape(n, S5_NKT, S5_KT, S5_SPT)

    def embed_out(c):
        t = c.astype(F32).reshape(n, S5_NKT, S5_GPT, S5_M, S5_P)
        return jnp.einsum("nkgmp,gh->nkhpgm", t, eye).reshape(n, S5_NKT, S5_SPT, S5_KT)

    pick = lambda a: a.reshape(n, S5_G, S5_M, S5_P)[:, :, 0, :].reshape(n, 1, S5_STATE)
    return {
        "ab_re": pick(ab_re), "ab_im": pick(ab_im),
        "bb": jnp.concatenate([embed_in(bb_re), embed_in(bb_im)], axis=-1).astype(BF16),
        "c_re": embed_out(c_re).astype(BF16), "c_im": embed_out(c_im).astype(BF16),
        "d_skip": d_skip.reshape(n, 1, D_MODEL).astype(F32),
        "w_glu": w_glu.astype(BF16), "w_out": w_out.astype(BF16),
    }


def _pad_rows(a, n):
    return jnp.pad(a, ((0, 0), (0, n - a.shape[1]), (0, 0)))


def _run_trunk(x, off, pos, mod, mod3, ffw, evw, s5w, ln_g, ln_b, gdn0, conv0, s5re0, s5im0, past, cfg):
    nb, T, _ = x.shape
    rope = _rope_tables(pos)
    new_ckv, new_kr, new_gdn, new_conv, new_re, new_im = [], [], [], [], [], []
    stream = off // nb
    for l in range(DEPTH):
        lng = lambda s: ln_g[l, s].reshape(1, D_MODEL)
        lnb = lambda s: ln_b[l, s].reshape(1, D_MODEL)
        x = _ffn_call(x, mod3, l, 0, off, *ffw[l][0], lng(0), lnb(0), cfg["ffn_bb"], cfg["ffn_tt"])
        i = l // 2
        if l % 2 == 0:
            ew = evw[i]
            conv0p = jnp.pad(conv0[i].astype(F32), ((0, 0), (SUBLANE - (GDN_CONV - 1), 0), (0, 0)))
            qkvn, z, bg, qa, ka, vv, ckv, kr, tail = _e1_call(
                x, mod3, l, off, ew, conv0p, rope, cfg["e1_bb"], cfg["e1_tt"], past is None)
            n_valid = min(T, GDN_C)
            if T < GDN_C:
                qkvn, z, bg = (_pad_rows(a, GDN_C) for a in (qkvn, z, bg))
            og, s_new = _gdn_call(qkvn, z, bg, gdn0[i].astype(F32), ew, cfg["gdn_bb"], n_valid)
            og = og[:, :T]
            if past is None:
                x = _mla_prompt_call(qa, ka, vv, og, x, mod3, l, off, ew["w_out"], lng(1), lnb(1), cfg["mla_tq"])
            else:
                x = _mla_sample_call(qa, past[0][i], past[1][i], ka, vv, og, x, mod3, l, off, ew, lng(1), lnb(1))
            new_gdn.append(s_new)
            new_conv.append(tail[:, SUBLANE - (GDN_CONV - 1):])
            new_ckv.append(ckv)
            new_kr.append(kr)
        else:
            sw = {k: v[i] for k, v in s5w.items()}
            x, h_re, h_im = _s5_call(x, mod, l, stream, s5re0[i].reshape(nb, S5_STATE).astype(F32),
                                     s5im0[i].reshape(nb, S5_STATE).astype(F32), sw, lng(1), lnb(1), cfg["s5_tt"])
            new_re.append(h_re.reshape(nb, S5_G, S5_P))
            new_im.append(h_im.reshape(nb, S5_G, S5_P))
        x = _ffn_call(x, mod3, l, 2, off, *ffw[l][1], lng(2), lnb(2), cfg["ffn_bb"], cfg["ffn_tt"])
    return (x, jnp.stack(new_ckv), jnp.stack(new_kr), jnp.stack(new_gdn), jnp.stack(new_conv),
            jnp.stack(new_re), jnp.stack(new_im))


def _config(nb, T):
    if T >= 512:
        return {"ffn_bb": 1, "ffn_tt": FF_ROWS, "e1_bb": 1, "e1_tt": 512, "gdn_bb": min(nb, 4), "mla_tq": 512,
                "s5_tt": 32}
    bb = max(1, min(nb, 512 // T))
    return {"ffn_bb": max(1, min(nb, FF_ROWS // T)), "ffn_tt": T, "e1_bb": bb, "e1_tt": T, "gdn_bb": min(nb, 4),
            "mla_tq": T, "s5_tt": min(T, 32)}


def kernel(x_prompt, x_sample, c_prompt, c_sample, cache_mla_latent, cache_mla_krope, state_gdn, state_gdn_conv, state_s5_re, state_s5_im, w_ada, b_ada, ln_g, ln_b, w_ff_in, w_ff_out, w_mix_in, gdn_conv_w, gdn_a_log, gdn_dt_bias, gdn_norm_w, mla_q_norm, mla_w_q_b, mla_kv_norm, mla_w_kv_b, w_mix_out, s5_lambda_re, s5_lambda_im, s5_log_dt, s5_b_re, s5_b_im, s5_c_re, s5_c_im, s5_d, s5_w_glu, s5_w_out):
    bp, sp, _ = x_prompt.shape
    bs, ss, _ = x_sample.shape
    assert bp == bs, "the modulation table is indexed in blocks of one stream's batch"

    mod = _ada_call(jnp.concatenate([c_prompt, c_sample], axis=0), w_ada, b_ada)
    mod3 = mod.reshape(DEPTH * (bp + bs), 1, -1)

    ffw = [[_prep_ffn(w_ff_in[l, s], w_ff_out[l, s]) for s in range(2)] for l in range(DEPTH)]
    evw = [_prep_even(w_mix_in[i], gdn_conv_w[i], gdn_a_log[i], gdn_dt_bias[i], gdn_norm_w[i], mla_q_norm[i],
                      mla_w_q_b[i], mla_kv_norm[i], mla_w_kv_b[i], w_mix_out[i]) for i in range(N_EVEN)]
    s5w = _prep_s5(s5_lambda_re, s5_lambda_im, s5_log_dt, s5_b_re, s5_b_im, s5_c_re, s5_c_im, s5_d,
                   s5_w_glu, s5_w_out)

    zeros = lambda *s: jnp.zeros(s, F32)
    outs_p = _run_trunk(
        x_prompt, 0, jnp.arange(sp, dtype=jnp.int32), mod, mod3, ffw, evw, s5w, ln_g, ln_b,
        zeros(N_EVEN, bp, GDN_H, GDN_DK, GDN_DV), zeros(N_EVEN, bp, GDN_CONV - 1, GDN_QKV),
        zeros(N_ODD, bp, S5_G, S5_P), zeros(N_ODD, bp, S5_G, S5_P), None, _config(bp, sp))

    n_past = cache_mla_latent.shape[2]
    past_kr = jnp.pad(cache_mla_krope, ((0, 0), (0, 0), (0, 0), (0, LANE - MLA_ROPE)))
    outs_s = _run_trunk(
        x_sample, bp, n_past + jnp.arange(ss, dtype=jnp.int32), mod, mod3, ffw, evw, s5w, ln_g, ln_b,
        state_gdn, state_gdn_conv, state_s5_re, state_s5_im, (cache_mla_latent, past_kr), _config(bs, ss))

    y_p, lat_p, kr_p, gdn_p, conv_p, re_p, im_p = outs_p
    y_s, lat_s, kr_s, gdn_s, conv_s, re_s, im_s = outs_s
    return (y_p, y_s, lat_p, kr_p, gdn_p, conv_p, re_p, im_p, lat_s, kr_s, gdn_s, conv_s, re_s, im_s)
```
